```python
import math
import jax, jax.numpy as jnp
from jax import lax
import numpy as np

D_MODEL = 1024
BATCH = 8
SEQ = 8192
DEPTH = 2

MEM_TOKENS = 256
ROPE_THETA = 500000.0
Q_BLOCK = 128
EPS = 1e-6
CONV_CH = 512
CONV_WIDTH = 3
DIFF_HEADS = 4
DIFF_HEAD_DIM = 64
DIFF_V_DIM = 2 * DIFF_HEAD_DIM
DIFF_ROT = DIFF_HEAD_DIM // 4
MLA_HEADS = 4
MLA_Q_RANK = 256
MLA_KV_RANK = 128
MLA_NOPE = 64
MLA_ROPE = 32
MLA_V = 128
N_BRANCH = 3
BRANCH_W = 512
CROSS_HEADS = 4
CROSS_HEAD_DIM = D_MODEL // CROSS_HEADS
PEER_HEADS = 8
PEER_N_KEYS = 128
PEER_EXPERTS = PEER_N_KEYS * PEER_N_KEYS
PEER_KEY_DIM = 128
PEER_TOPK = 16
PEER_CHUNK = 32
SPLITS = (CONV_CH, CONV_CH, CONV_CH,
          DIFF_HEADS * 2 * DIFF_HEAD_DIM, DIFF_HEADS * 2 * DIFF_HEAD_DIM, DIFF_HEADS * DIFF_V_DIM,
          MLA_Q_RANK, MLA_KV_RANK, MLA_ROPE,
          N_BRANCH * D_MODEL)
IN_W = sum(SPLITS)

kernel_name = 'hybrid_gated_conv_diffattn_mla_peer_encoder'


def _rmsnorm(x, g):
    xf = x.astype(jnp.float32)
    y = xf * lax.rsqrt(jnp.mean(xf * xf, axis=-1, keepdims=True) + EPS)
    return (y * g.astype(jnp.float32)).astype(x.dtype)


def _rope_tables(positions, rot_dim):
    inv = 1.0 / (ROPE_THETA ** (jnp.arange(0, rot_dim, 2, dtype=jnp.float32) / rot_dim))
    ang = positions.astype(jnp.float32)[..., None] * inv
    return jnp.cos(ang)[:, :, None, :], jnp.sin(ang)[:, :, None, :]


def _apply_rope(t, cos, sin):
    half = cos.shape[-1]
    rd = 2 * half
    c = cos.astype(t.dtype)
    s = sin.astype(t.dtype)
    t1 = t[..., :half]
    t2 = t[..., half:rd]
    return jnp.concatenate([t1 * c - t2 * s, t2 * c + t1 * s, t[..., rd:]], axis=-1)


def _blocks(t, size):
    b, s = t.shape[:2]
    return jnp.moveaxis(t.reshape(b, s // size, size, *t.shape[2:]), 1, 0)


def _unblocks(t):
    nb, b, size = t.shape[:3]
    return jnp.moveaxis(t, 0, 1).reshape(b, nb * size, *t.shape[3:])


def _probs(q, k, scale):
    s = jnp.einsum('bqhd,bkhd->bhqk', q, k).astype(jnp.float32) * scale
    return jax.nn.softmax(s, axis=-1)


def _short_conv_branch(a_b, a_c, a_h, conv_w):
    z = a_c * a_h
    z = lax.conv_general_dilated(
        z, conv_w[:, None, :].astype(z.dtype), window_strides=(1,),
        padding=[(CONV_WIDTH // 2, CONV_WIDTH // 2)],
        dimension_numbers=('NWC', 'WIO', 'NWC'), feature_group_count=CONV_CH)
    return a_b * z


def _diff_attention(dq, dk, dv, lam, lam_init, subln, cos, sin):
    b, s = dq.shape[:2]
    q = dq.reshape(b, s, DIFF_HEADS, 2, DIFF_HEAD_DIM)
    k = dk.reshape(b, s, DIFF_HEADS, 2, DIFF_HEAD_DIM)
    q1 = _apply_rope(q[..., 0, :], cos, sin)
    q2 = _apply_rope(q[..., 1, :], cos, sin)
    k1 = _apply_rope(k[..., 0, :], cos, sin)
    k2 = _apply_rope(k[..., 1, :], cos, sin)
    v = dv.reshape(b, s, DIFF_HEADS, DIFF_V_DIM)
    scale = DIFF_HEAD_DIM ** -0.5

    def block(qs):
        q1b, q2b = qs
        a = _probs(q1b, k1, scale) - lam * _probs(q2b, k2, scale)
        return jnp.einsum('bhqk,bkhe->bqhe', a.astype(v.dtype), v)

    o = _unblocks(lax.map(block, (_blocks(q1, Q_BLOCK), _blocks(q2, Q_BLOCK))))
    o = _rmsnorm(o, subln) * (1.0 - lam_init)
    return o.reshape(b, s, DIFF_HEADS * DIFF_V_DIM)


def _mla(cq, ckv, kr, q_norm, w_uq, kv_norm, w_ukv, cos, sin):
    b, s = cq.shape[:2]
    q = (_rmsnorm(cq, q_norm) @ w_uq).reshape(b, s, MLA_HEADS, MLA_NOPE + MLA_ROPE)
    q = jnp.concatenate([q[..., :MLA_NOPE], _apply_rope(q[..., MLA_NOPE:], cos, sin)], axis=-1)
    kv = (_rmsnorm(ckv, kv_norm) @ w_ukv).reshape(b, s, MLA_HEADS, MLA_NOPE + MLA_V)
    k_rope = jnp.broadcast_to(_apply_rope(kr[:, :, None, :], cos, sin), (b, s, MLA_HEADS, MLA_ROPE))
    k = jnp.concatenate([kv[..., :MLA_NOPE], k_rope], axis=-1)
    v = kv[..., MLA_NOPE:]
    scale = (MLA_NOPE + MLA_ROPE) ** -0.5

    def block(qb):
        return jnp.einsum('bhqk,bkhe->bqhe', _probs(qb, k, scale).astype(v.dtype), v)

    o = _unblocks(lax.map(block, _blocks(q, Q_BLOCK)))
    return o.reshape(b, s, MLA_HEADS * MLA_V)


def _hybrid_mixer(xn, w_in, conv_w, lam, lam_init, subln, q_norm, w_uq, kv_norm, w_ukv,
                  w_branch, w_out, cos_d, sin_d, cos_m, sin_m):
    b, s, _ = xn.shape
    proj = xn @ w_in
    offs = [int(o) for o in np.cumsum(SPLITS)[:-1]]
    a_b, a_c, a_h, dq, dk, dv, cq, ckv, kr, gates = jnp.split(proj, offs, axis=-1)
    y_a = _short_conv_branch(a_b, a_c, a_h, conv_w)
    y_b = _diff_attention(dq, dk, dv, lam, lam_init, subln, cos_d, sin_d)
    y_c = _mla(cq, ckv, kr, q_norm, w_uq, kv_norm, w_ukv, cos_m, sin_m)
    g = jax.nn.sigmoid(gates.reshape(b, s, N_BRANCH, D_MODEL))
    merged = (g[:, :, 0] * (y_a @ w_branch[0])
              + g[:, :, 1] * (y_b @ w_branch[1])
              + g[:, :, 2] * (y_c @ w_branch[2]))
    return merged @ w_out


def _memory_cross_attention(hn, memn, w_cq, w_ckv, w_co):
    b, s, _ = hn.shape
    m = memn.shape[1]
    q = (hn @ w_cq).reshape(b, s, CROSS_HEADS, CROSS_HEAD_DIM)
    kv = (memn @ w_ckv).reshape(b, m, 2, CROSS_HEADS, CROSS_HEAD_DIM)
    k = kv[:, :, 0]
    v = kv[:, :, 1]
    p = _probs(q, k, CROSS_HEAD_DIM ** -0.5)
    o = jnp.einsum('bhsm,bmhe->bshe', p.astype(v.dtype), v).reshape(b, s, D_MODEL)
    return o @ w_co


def _peer(xn, w_q, keys, u_tab, v_tab):
    b, seq, _ = xn.shape
    q = (xn @ w_q).reshape(b, seq, PEER_HEADS, 2, PEER_KEY_DIM)
    sc = jnp.einsum('bshpd,hpnd->bshpn', q, keys).astype(jnp.float32)
    s1, i1 = lax.top_k(sc[..., 0, :], PEER_TOPK)
    s2, i2 = lax.top_k(sc[..., 1, :], PEER_TOPK)
    n_cand = PEER_TOPK * PEER_TOPK
    cand_s = (s1[..., :, None] + s2[..., None, :]).reshape(b, seq, PEER_HEADS, n_cand)
    cand_i = (i1[..., :, None] * PEER_N_KEYS + i2[..., None, :]).reshape(b, seq, PEER_HEADS, n_cand)
    top_s, pos = lax.top_k(cand_s, PEER_TOPK)
    idx = jnp.take_along_axis(cand_i, pos, axis=-1).reshape(b, seq, PEER_HEADS * PEER_TOPK)
    gate = jax.nn.softmax(top_s, axis=-1).reshape(b, seq, PEER_HEADS * PEER_TOPK).astype(xn.dtype)

    def chunk(args):
        xb, ib, gb = args
        act = jax.nn.gelu(jnp.einsum('bcd,bcjd->bcj', xb, jnp.take(u_tab, ib, axis=0)),
                          approximate=False)
        return jnp.einsum('bcj,bcjd->bcd', gb * act, jnp.take(v_tab, ib, axis=0))

    out = lax.map(chunk, (_blocks(xn, PEER_CHUNK), _blocks(idx, PEER_CHUNK), _blocks(gate, PEER_CHUNK)))
    return _unblocks(out)


def setup_inputs(seed: int = 0) -> dict:
    key = jax.random.key(seed)
    ks = jax.random.split(key, 26)
    L, D = DEPTH, D_MODEL

    def nrm(k, shape, scale):
        return jax.random.normal(k, shape, dtype=jnp.float32) * scale

    def gain(k, shape):
        return 1.0 + 0.02 * jax.random.normal(k, shape, dtype=jnp.float32)

    positions = (jnp.arange(SEQ, dtype=jnp.int32)[None, :]
                 + jax.random.randint(ks[2], (BATCH, 1), 0, 4096, dtype=jnp.int32))
    return {
        'x': nrm(ks[0], (BATCH, SEQ, D), 1.0),
        'mem': nrm(ks[1], (BATCH, MEM_TOKENS, D), 1.0),
        'positions': positions,
        'norm_mix': gain(ks[3], (L, D)),
        'w_in': nrm(ks[4], (L, D, IN_W), D ** -0.5),
        'conv_w': nrm(ks[5], (L, CONV_WIDTH, CONV_CH), CONV_WIDTH ** -0.5),
        'diff_lambda': nrm(ks[6], (L, 4, DIFF_HEAD_DIM), 0.1),
        'diff_subln': gain(ks[7], (L, DIFF_V_DIM)),
        'mla_q_norm': gain(ks[8], (L, MLA_Q_RANK)),
        'mla_w_uq': nrm(ks[9], (L, MLA_Q_RANK, MLA_HEADS * (MLA_NOPE + MLA_ROPE)), MLA_Q_RANK ** -0.5),
        'mla_kv_norm': gain(ks[10], (L, MLA_KV_RANK)),
        'mla_w_ukv': nrm(ks[11], (L, MLA_KV_RANK, MLA_HEADS * (MLA_NOPE + MLA_V)), MLA_KV_RANK ** -0.5),
        'w_branch': nrm(ks[12], (L, N_BRANCH, BRANCH_W, D), BRANCH_W ** -0.5),
        'w_out': nrm(ks[13], (L, D, D), D ** -0.5),
        'norm_cross': gain(ks[14], (L, D)),
        'norm_mem': gain(ks[15], (L, D)),
        'w_cq': nrm(ks[16], (L, D, D), D ** -0.5),
        'w_ckv': nrm(ks[17], (L, D, 2 * D), D ** -0.5),
        'w_co': nrm(ks[18], (L, D, D), D ** -0.5),
        'norm_ffn': gain(ks[19], (L, D)),
        'peer_w_q': nrm(ks[20], (L, D, PEER_HEADS * 2 * PEER_KEY_DIM), D ** -0.5),
        'peer_keys': nrm(ks[21], (L, PEER_HEADS, 2, PEER_N_KEYS, PEER_KEY_DIM), PEER_KEY_DIM ** -0.5),
        'peer_u': nrm(ks[22], (L, PEER_EXPERTS, D), D ** -0.5),
        'peer_v': nrm(ks[23], (L, PEER_EXPERTS, D), PEER_HEADS ** -0.5),
        'final_norm': gain(ks[24], (D,)),
    }


def reference(x, mem, positions, norm_mix, w_in, conv_w, diff_lambda, diff_subln,
              mla_q_norm, mla_w_uq, mla_kv_norm, mla_w_ukv, w_branch, w_out,
              norm_cross, norm_mem, w_cq, w_ckv, w_co, norm_ffn,
              peer_w_q, peer_keys, peer_u, peer_v, final_norm):
    cos_d, sin_d = _rope_tables(positions, DIFF_ROT)
    cos_m, sin_m = _rope_tables(positions, MLA_ROPE)
    h = x
    for l in range(DEPTH):
        lam_init = 0.8 - 0.6 * math.exp(-0.3 * l)
        lv = diff_lambda[l].astype(jnp.float32)
        lam = jnp.exp(jnp.sum(lv[0] * lv[1])) - jnp.exp(jnp.sum(lv[2] * lv[3])) + lam_init
        h = h + _hybrid_mixer(_rmsnorm(h, norm_mix[l]), w_in[l], conv_w[l], lam, lam_init,
                              diff_subln[l], mla_q_norm[l], mla_w_uq[l], mla_kv_norm[l],
                              mla_w_ukv[l], w_branch[l], w_out[l], cos_d, sin_d, cos_m, sin_m)
        h = h + _memory_cross_attention(_rmsnorm(h, norm_cross[l]), _rmsnorm(mem, norm_mem[l]),
                                        w_cq[l], w_ckv[l], w_co[l])
        h = h + _peer(_rmsnorm(h, norm_ffn[l]), peer_w_q[l], peer_keys[l], peer_u[l], peer_v[l])
    return _rmsnorm(h, final_norm)
```

```python
import functools
import math

import numpy as np
import jax
import jax.numpy as jnp
from jax import lax
from jax.experimental import pallas as pl
from jax.experimental.pallas import tpu as pltpu

EPS = 1e-6
ROPE_THETA = 500000.0
CONV_CH = 512
DIFF_HEADS = 4
DIFF_HEAD_DIM = 64
DIFF_ROT = DIFF_HEAD_DIM // 4
MLA_HEADS = 4
MLA_Q_RANK = 256
MLA_KV_RANK = 128
MLA_NOPE = 64
MLA_ROPE = 32
MLA_V = 128
N_BRANCH = 3
CROSS_HEADS = 4
PEER_HEADS = 8
PEER_N_KEYS = 128
PEER_KEY_DIM = 128
PEER_TOPK = 16

LANES = 128
SUBLANES = 8
VMEM_LIMIT_BYTES = 56 * 1024 * 1024

HEAD_W = 128
OFF_GATES = 0
OFF_AB = 3072
OFF_AC = OFF_AB + CONV_CH
OFF_AH = OFF_AC + CONV_CH
OFF_DQ = OFF_AH + CONV_CH
OFF_DK = OFF_DQ + 512
OFF_DV = OFF_DK + 512
OFF_CQ = OFF_DV + 512
OFF_CKV = OFF_CQ + MLA_Q_RANK
OFF_KR = OFF_CKV + MLA_KV_RANK
PROJ_W = OFF_KR + LANES


def _cparams(*sem):
    return pltpu.CompilerParams(dimension_semantics=sem, vmem_limit_bytes=VMEM_LIMIT_BYTES)


def _norm_matmul_kernel(x_ref, g_ref, w_ref, o_ref, xn_ref):
    @pl.when(pl.program_id(1) == 0)
    def _():
        x = x_ref[...]
        ms = jnp.mean(x * x, axis=-1, keepdims=True)
        xn_ref[...] = (x * lax.rsqrt(ms + EPS) * g_ref[...]).astype(xn_ref.dtype)

    o_ref[...] = jnp.dot(xn_ref[...], w_ref[...],
                         preferred_element_type=jnp.float32).astype(o_ref.dtype)


def _norm_matmul(x, g, w, *, tm, tn, name):
    t, k = x.shape
    n = w.shape[1]
    tm = min(tm, t)
    return pl.pallas_call(
        _norm_matmul_kernel,
        out_shape=jax.ShapeDtypeStruct((t, n), jnp.bfloat16),
        grid=(t // tm, n // tn),
        in_specs=[pl.BlockSpec((tm, k), lambda i, j: (i, 0)),
                  pl.BlockSpec((1, k), lambda i, j: (0, 0)),
                  pl.BlockSpec((k, tn), lambda i, j: (0, j))],
        out_specs=pl.BlockSpec((tm, tn), lambda i, j: (i, j)),
        scratch_shapes=[pltpu.VMEM((tm, k), jnp.bfloat16)],
        compiler_params=_cparams("parallel", "arbitrary"),
        name=name,
    )(x, g.reshape(1, k), w)


def _rope_block(xb, c, s, lo_mask, shift):
    partner = jnp.where(lo_mask, pltpu.roll(xb, LANES - shift, axis=1), pltpu.roll(xb, shift, axis=1))
    return xb * c + partner * s


def _rms(x, g):
    ms = jnp.mean(x * x, axis=-1, keepdims=True)
    return x * lax.rsqrt(ms + EPS) * g


def _prep_kernel(dq_ref, dk_ref, cq_ref, ckv_ref, kr_ref, cd_ref, sd_ref, cm_ref, sm_ref,
                 qn_ref, wuq_ref, kvn_ref, wk_ref, wv_ref, place_ref,
                 dq_o, dk_o, mq_o, mk_o, mv_o):
    f32 = jnp.float32
    tm = dq_ref.shape[0]
    lane = lax.broadcasted_iota(jnp.int32, (tm, LANES), 1)
    cd, sd, cm, sm = cd_ref[...], sd_ref[...], cm_ref[...], sm_ref[...]
    d_lo = (lane % DIFF_HEAD_DIM) < (DIFF_ROT // 2)
    m_lo = lane < (MLA_NOPE + MLA_ROPE // 2)
    d_scale = DIFF_HEAD_DIM ** -0.5
    m_scale = (MLA_NOPE + MLA_ROPE) ** -0.5
    first_half = lane < DIFF_HEAD_DIM

    for h in range(DIFF_HEADS):
        sl = slice(h * HEAD_W, (h + 1) * HEAD_W)
        qb = _rope_block(dq_ref[:, sl].astype(f32), cd, sd, d_lo, DIFF_ROT // 2) * d_scale
        kb = _rope_block(dk_ref[:, sl].astype(f32), cd, sd, d_lo, DIFF_ROT // 2)
        dq_o[:, 2 * h * HEAD_W:(2 * h + 1) * HEAD_W] = jnp.where(first_half, qb, 0.0).astype(dq_o.dtype)
        dq_o[:, (2 * h + 1) * HEAD_W:(2 * h + 2) * HEAD_W] = jnp.where(first_half, 0.0, qb).astype(dq_o.dtype)
        dk_o[:, sl] = kb.astype(dk_o.dtype)

    cqn = _rms(cq_ref[...].astype(f32), qn_ref[...]).astype(jnp.bfloat16)
    q = jnp.dot(cqn, wuq_ref[...], preferred_element_type=f32)
    ckvn = _rms(ckv_ref[...].astype(f32), kvn_ref[...]).astype(jnp.bfloat16)
    k = jnp.dot(ckvn, wk_ref[...], preferred_element_type=f32)
    k = k + jnp.dot(kr_ref[...], place_ref[...], preferred_element_type=f32)
    v = jnp.dot(ckvn, wv_ref[...], preferred_element_type=f32)
    for h in range(MLA_HEADS):
        sl = slice(h * HEAD_W, (h + 1) * HEAD_W)
        mq_o[:, sl] = (_rope_block(q[:, sl], cm, sm, m_lo, MLA_ROPE // 2) * m_scale).astype(mq_o.dtype)
        mk_o[:, sl] = _rope_block(k[:, sl], cm, sm, m_lo, MLA_ROPE // 2).astype(mk_o.dtype)
    mv_o[...] = v.astype(mv_o.dtype)


def _prep(proj, cd, sd, cm, sm, qn, wuq, kvn, wk, wv, place, *, tm):
    t = proj.shape[0]
    tm = min(tm, t)
    row = lambda w, off: pl.BlockSpec((tm, w), lambda i, _o=off // w: (i, _o))
    full = lambda a: pl.BlockSpec(a.shape, lambda i: (0,) * a.ndim)
    tab = pl.BlockSpec((tm, LANES), lambda i: (i, 0))
    bf = jnp.bfloat16
    return pl.pallas_call(
        _prep_kernel,
        out_shape=[jax.ShapeDtypeStruct((t, 2 * DIFF_HEADS * HEAD_W), bf),
                   jax.ShapeDtypeStruct((t, DIFF_HEADS * HEAD_W), bf),
                   jax.ShapeDtypeStruct((t, MLA_HEADS * HEAD_W), bf),
                   jax.ShapeDtypeStruct((t, MLA_HEADS * HEAD_W), bf),
                   jax.ShapeDtypeStruct((t, MLA_HEADS * HEAD_W), bf)],
        grid=(t // tm,),
        in_specs=[row(512, OFF_DQ), row(512, OFF_DK), row(MLA_Q_RANK, OFF_CQ),
                  row(MLA_KV_RANK, OFF_CKV), row(LANES, OFF_KR), tab, tab, tab, tab,
                  full(qn), full(wuq), full(kvn), full(wk), full(wv), full(place)],
        out_specs=[pl.BlockSpec((tm, 2 * DIFF_HEADS * HEAD_W), lambda i: (i, 0)),
                   pl.BlockSpec((tm, DIFF_HEADS * HEAD_W), lambda i: (i, 0)),
                   pl.BlockSpec((tm, MLA_HEADS * HEAD_W), lambda i: (i, 0)),
                   pl.BlockSpec((tm, MLA_HEADS * HEAD_W), lambda i: (i, 0)),
                   pl.BlockSpec((tm, MLA_HEADS * HEAD_W), lambda i: (i, 0))],
        compiler_params=_cparams("parallel"),
        name="attn_prep",
    )(proj, proj, proj, proj, proj, cd, sd, cm, sm, qn, wuq, kvn, wk, wv, place)


def _flash_kernel(*refs, n_maps, tk, lam_init):
    if n_maps == 2:
        q_ref, k_ref, v_ref, lamv_ref, subln_ref, o_ref, m_ref, l_ref, acc_ref = refs
    else:
        q_ref, k_ref, v_ref, o_ref, m_ref, l_ref, acc_ref = refs
    f32 = jnp.float32
    tq = q_ref.shape[0]
    nk = k_ref.shape[0] // tk
    m_ref[...] = jnp.full(m_ref.shape, -jnp.inf, f32)
    l_ref[...] = jnp.zeros(l_ref.shape, f32)
    acc_ref[...] = jnp.zeros(acc_ref.shape, f32)

    def body(c, carry):
        start = pl.multiple_of(c * tk, tk)
        kc = k_ref[pl.ds(start, tk), :]
        vc = v_ref[pl.ds(start, tk), :]
        for mi in range(n_maps):
            q = q_ref[:, mi * HEAD_W:(mi + 1) * HEAD_W]
            s = lax.dot_general(q, kc, (((1,), (1,)), ((), ())), preferred_element_type=f32)
            m_prev = m_ref[mi]
            m_new = jnp.maximum(m_prev, jnp.max(s, axis=-1, keepdims=True))
            alpha = jnp.exp(m_prev - m_new)
            p = jnp.exp(s - m_new[:, :1])
            l_ref[mi] = alpha * l_ref[mi] + jnp.sum(p, axis=-1, keepdims=True)
            acc_ref[mi] = alpha * acc_ref[mi] + jnp.dot(p.astype(vc.dtype), vc, preferred_element_type=f32)
            m_ref[mi] = m_new
        return carry

    lax.fori_loop(0, nk, body, 0)

    if n_maps == 2:
        lv = lamv_ref[...]
        lam = (jnp.exp(jnp.sum(lv[0:1] * lv[1:2], axis=-1, keepdims=True))
               - jnp.exp(jnp.sum(lv[2:3] * lv[3:4], axis=-1, keepdims=True)) + lam_init)
        o = acc_ref[0] / l_ref[0] - lam * (acc_ref[1] / l_ref[1])
        o = _rms(o, subln_ref[...]) * (1.0 - lam_init)
    else:
        o = acc_ref[0] / l_ref[0]
    o_ref[...] = o.astype(o_ref.dtype)


def _flash(q, k, v, v_col0, batch, seq, heads, *, n_maps, tq, tk, lam_init=0.0, lamv=None, subln=None, name):
    tq = min(tq, seq)
    tk = min(tk, seq)
    nq = seq // tq
    vb = v_col0 // HEAD_W
    in_specs = [pl.BlockSpec((tq, n_maps * HEAD_W), lambda b, h, i: (b * nq + i, h)),
                pl.BlockSpec((seq, HEAD_W), lambda b, h, i: (b, h)),
                pl.BlockSpec((seq, HEAD_W), lambda b, h, i: (b, vb + h))]
    args = [q, k, v]
    if n_maps == 2:
        in_specs += [pl.BlockSpec(lamv.shape, lambda b, h, i: (0, 0)),
                     pl.BlockSpec(subln.shape, lambda b, h, i: (0, 0))]
        args += [lamv, subln]
    return pl.pallas_call(
        functools.partial(_flash_kernel, n_maps=n_maps, tk=tk, lam_init=lam_init),
        out_shape=jax.ShapeDtypeStruct((batch * seq, heads * HEAD_W), jnp.bfloat16),
        grid=(batch, heads, nq),
        in_specs=in_specs,
        out_specs=pl.BlockSpec((tq, HEAD_W), lambda b, h, i: (b * nq + i, h)),
        scratch_shapes=[pltpu.VMEM((n_maps, tq, HEAD_W), jnp.float32),
                        pltpu.VMEM((n_maps, tq, HEAD_W), jnp.float32),
                        pltpu.VMEM((n_maps, tq, HEAD_W), jnp.float32)],
        compiler_params=_cparams("parallel", "parallel", "arbitrary"),
        name=name,
    )(*args)


def _sigmoid(x):
    return 1.0 / (1.0 + jnp.exp(-x))


def _merge_kernel(g0_ref, g1_ref, g2_ref, ab_ref, ac_ref, ah_ref, pc_ref, ph_ref, nc_ref, nh_ref,
                  yb_ref, yc_ref, h_ref, cw_ref, wb_ref, wo_ref, o_ref, *, tiles_per_seq):
    f32 = jnp.float32
    tm = ab_ref.shape[0]
    i = pl.program_id(0)
    pos = i % tiles_per_seq
    z = ac_ref[...].astype(f32) * ah_ref[...].astype(f32)
    zp_edge = pc_ref[SUBLANES - 1:SUBLANES, :].astype(f32) * ph_ref[SUBLANES - 1:SUBLANES, :].astype(f32)
    zn_edge = nc_ref[0:1, :].astype(f32) * nh_ref[0:1, :].astype(f32)
    zp_edge = jnp.where(pos > 0, zp_edge, 0.0)
    zn_edge = jnp.where(pos < tiles_per_seq - 1, zn_edge, 0.0)
    row = lax.broadcasted_iota(jnp.int32, z.shape, 0)
    zp = jnp.where(row == 0, zp_edge, pltpu.roll(z, 1, axis=0))
    zn = jnp.where(row == tm - 1, zn_edge, pltpu.roll(z, tm - 1, axis=0))
    cw = cw_ref[...]
    y_a = ab_ref[...].astype(f32) * (cw[0:1] * zp + cw[1:2] * z + cw[2:3] * zn)

    bf = jnp.bfloat16
    merged = _sigmoid(g0_ref[...].astype(f32)) * jnp.dot(y_a.astype(bf), wb_ref[0], preferred_element_type=f32)
    merged += _sigmoid(g1_ref[...].astype(f32)) * jnp.dot(yb_ref[...], wb_ref[1], preferred_element_type=f32)
    merged += _sigmoid(g2_ref[...].astype(f32)) * jnp.dot(yc_ref[...], wb_ref[2], preferred_element_type=f32)
    o_ref[...] = h_ref[...] + jnp.dot(merged.astype(bf), wo_ref[...], preferred_element_type=f32)


def _merge(proj, y_b, y_c, h, conv_w, w_branch, w_out, seq, *, tm):
    t, d = h.shape
    tm = min(tm, seq)
    tiles_per_seq = seq // tm
    r8 = tm // SUBLANES
    nrow8 = t // SUBLANES
    row = lambda w, off: pl.BlockSpec((tm, w), lambda i, _o=off // w: (i, _o))
    prev = lambda off: pl.BlockSpec((SUBLANES, CONV_CH),
                                    lambda i, _o=off // CONV_CH: (jnp.maximum(i * r8 - 1, 0), _o))
    nxt = lambda off: pl.BlockSpec((SUBLANES, CONV_CH),
                                   lambda i, _o=off // CONV_CH: (jnp.minimum((i + 1) * r8, nrow8 - 1), _o))
    full = lambda a: pl.BlockSpec(a.shape, lambda i: (0,) * a.ndim)
    return pl.pallas_call(
        functools.partial(_merge_kernel, tiles_per_seq=tiles_per_seq),
        out_shape=jax.ShapeDtypeStruct((t, d), jnp.float32),
        grid=(t // tm,),
        in_specs=[row(d, OFF_GATES), row(d, OFF_GATES + d), row(d, OFF_GATES + 2 * d),
                  row(CONV_CH, OFF_AB), row(CONV_CH, OFF_AC), row(CONV_CH, OFF_AH),
                  prev(OFF_AC), prev(OFF_AH), nxt(OFF_AC), nxt(OFF_AH),
                  pl.BlockSpec((tm, y_b.shape[1]), lambda i: (i, 0)),
                  pl.BlockSpec((tm, y_c.shape[1]), lambda i: (i, 0)),
                  pl.BlockSpec((tm, d), lambda i: (i, 0)),
                  full(conv_w), full(w_branch), full(w_out)],
        out_specs=pl.BlockSpec((tm, d), lambda i: (i, 0)),
        compiler_params=_cparams("parallel"),
        name="gated_merge",
    )(proj, proj, proj, proj, proj, proj, proj, proj, proj, proj, y_b, y_c, h, conv_w, w_branch, w_out)


def _cross_kernel(h_ref, g_ref, wq_ref, kv_ref, wo_ref, o_ref):
    f32, bf = jnp.float32, jnp.bfloat16
    h = h_ref[...]
    d = h.shape[1]
    hd = d // CROSS_HEADS
    hn = _rms(h, g_ref[...]).astype(bf)
    q = (jnp.dot(hn, wq_ref[...], preferred_element_type=f32) * (hd ** -0.5)).astype(bf)
    outs = []
    for hh in range(CROSS_HEADS):
        kh = kv_ref[:, hh * hd:(hh + 1) * hd]
        vh = kv_ref[:, d + hh * hd:d + (hh + 1) * hd]
        s = lax.dot_general(q[:, hh * hd:(hh + 1) * hd], kh, (((1,), (1,)), ((), ())),
                            preferred_element_type=f32)
        p = jnp.exp(s - jnp.max(s, axis=-1, keepdims=True))
        p = p / jnp.sum(p, axis=-1, keepdims=True)
        outs.append(jnp.dot(p.astype(bf), vh, preferred_element_type=f32).astype(bf))
    o = jnp.concatenate(outs, axis=-1)
    o_ref[...] = h + jnp.dot(o, wo_ref[...], preferred_element_type=f32)


def _cross(h, g, w_cq, memkv, w_co, seq, mem_tokens, *, tm):
    t, d = h.shape
    tm = min(tm, seq)
    tiles_per_seq = seq // tm
    full = lambda a: pl.BlockSpec(a.shape, lambda i: (0,) * a.ndim)
    return pl.pallas_call(
        _cross_kernel,
        out_shape=jax.ShapeDtypeStruct((t, d), jnp.float32),
        grid=(t // tm,),
        in_specs=[pl.BlockSpec((tm, d), lambda i: (i, 0)), full(g), full(w_cq),
                  pl.BlockSpec((mem_tokens, 2 * d), lambda i: (i // tiles_per_seq, 0)), full(w_co)],
        out_specs=pl.BlockSpec((tm, d), lambda i: (i, 0)),
        compiler_params=_cparams("parallel"),
        name="cross_attn",
    )(h, g, w_cq, memkv, w_co)


def _topk_rows(sc, k, vals_ref, idx_ref, payload=None):
    n = sc.shape[0]
    row = lax.broadcasted_iota(jnp.int32, sc.shape, 0).astype(jnp.float32)
    for r in range(k):
        m = jnp.max(sc, axis=0, keepdims=True)
        first = jnp.min(jnp.where(sc == m, row, float(n)), axis=0, keepdims=True)
        hit = row == first
        vals_ref[r:r + 1, :] = m
        if payload is None:
            idx_ref[r:r + 1, :] = first
        else:
            idx_ref[r:r + 1, :] = jnp.sum(jnp.where(hit, payload, 0.0), axis=0, keepdims=True)
        sc = jnp.where(hit, -jnp.inf, sc)


def _route_kernel(h_ref, g_ref, wq_ref, keys_ref, xn_o, idx_o, gate_o,
                  s1_ref, i1_ref, s2_ref, i2_ref, cs_ref, ci_ref, ts_ref, ti_ref):
    f32, bf = jnp.float32, jnp.bfloat16
    xn = _rms(h_ref[...], g_ref[...])
    xn_o[...] = xn
    q = jnp.dot(xn.astype(bf), wq_ref[...], preferred_element_type=f32).astype(bf)
    kd = PEER_KEY_DIM
    for hh in range(PEER_HEADS):
        for p, (s_ref, i_ref) in enumerate(((s1_ref, i1_ref), (s2_ref, i2_ref))):
            c0 = (hh * 2 + p) * kd
            sc = lax.dot_general(keys_ref[hh, p], q[:, c0:c0 + kd], (((1,), (1,)), ((), ())),
                                 preferred_element_type=f32)
            _topk_rows(sc, PEER_TOPK, s_ref, i_ref)
        s2 = s2_ref[...]
        i2 = i2_ref[...]
        for a in range(PEER_TOPK):
            cs_ref[a * PEER_TOPK:(a + 1) * PEER_TOPK, :] = s1_ref[a:a + 1, :] + s2
            ci_ref[a * PEER_TOPK:(a + 1) * PEER_TOPK, :] = i1_ref[a:a + 1, :] * float(PEER_N_KEYS) + i2
        _topk_rows(cs_ref[...], PEER_TOPK, ts_ref, ti_ref, payload=ci_ref[...])
        ts = ts_ref[...]
        e = jnp.exp(ts - ts[0:1, :])
        gate_o[hh * PEER_TOPK:(hh + 1) * PEER_TOPK, :] = e / jnp.sum(e, axis=0, keepdims=True)
        idx_o[hh * PEER_TOPK:(hh + 1) * PEER_TOPK, :] = ti_ref[...].astype(jnp.int32)


def _route(h, g, w_q, keys, *, tm):
    t, d = h.shape
    tm = min(tm, t)
    nj = PEER_HEADS * PEER_TOPK
    full = lambda a: pl.BlockSpec(a.shape, lambda i: (0,) * a.ndim)
    small = lambda: pltpu.VMEM((PEER_TOPK, tm), jnp.float32)
    big = lambda: pltpu.VMEM((PEER_TOPK * PEER_TOPK, tm), jnp.float32)
    return pl.pallas_call(
        _route_kernel,
        out_shape=[jax.ShapeDtypeStruct((t, d), jnp.float32),
                   jax.ShapeDtypeStruct((nj, t), jnp.int32),
                   jax.ShapeDtypeStruct((nj, t), jnp.float32)],
        grid=(t // tm,),
        in_specs=[pl.BlockSpec((tm, d), lambda i: (i, 0)), full(g), full(w_q), full(keys)],
        out_specs=[pl.BlockSpec((tm, d), lambda i: (i, 0)),
                   pl.BlockSpec((nj, tm), lambda i: (0, i)),
                   pl.BlockSpec((nj, tm), lambda i: (0, i))],
        scratch_shapes=[small(), small(), small(), small(), big(), big(), small(), small()],
        compiler_params=_cparams("parallel"),
        name="peer_route",
    )(h, g, w_q, keys)


def _gelu(x):
    return 0.5 * x * (1.0 + lax.erf(x * (2.0 ** -0.5)))


def _experts_kernel(idx_ref, gate_ref, xn_ref, h_ref, u_hbm, v_hbm, o_ref, ubuf, vbuf, sem):
    tt = xn_ref.shape[0]
    nj = gate_ref.shape[1]
    nrows = tt * nj

    def issue(r, carry):
        e = idx_ref[0, 0, r]
        pltpu.make_async_copy(u_hbm.at[pl.ds(e, 1)], ubuf.at[pl.ds(r, 1)], sem.at[0]).start()
        pltpu.make_async_copy(v_hbm.at[pl.ds(e, 1)], vbuf.at[pl.ds(r, 1)], sem.at[1]).start()
        return carry

    lax.fori_loop(0, nrows, issue, 0, unroll=8)
    pltpu.make_async_copy(u_hbm.at[pl.ds(0, nrows)], ubuf, sem.at[0]).wait()
    pltpu.make_async_copy(v_hbm.at[pl.ds(0, nrows)], vbuf, sem.at[1]).wait()

    nchunk = xn_ref.shape[1] // LANES
    for t in range(tt):
        acc = None
        for s in range(nchunk):
            us = ubuf[pl.ds(t * nj, nj), s, :]
            part = us * xn_ref[t:t + 1, s * LANES:(s + 1) * LANES]
            acc = part if acc is None else acc + part
        act = jnp.sum(acc, axis=-1, keepdims=True)
        w = gate_ref[0, :, t:t + 1] * _gelu(act)
        for s in range(nchunk):
            vs = vbuf[pl.ds(t * nj, nj), s, :]
            contrib = jnp.sum(vs * w, axis=0, keepdims=True)
            sl = slice(s * LANES, (s + 1) * LANES)
            o_ref[t:t + 1, sl] = h_ref[t:t + 1, sl] + contrib


def _experts(idx3, gate3, xn, h, u3, v3, *, tt):
    t, d = h.shape
    nj = gate3.shape[1]
    return pl.pallas_call(
        _experts_kernel,
        out_shape=jax.ShapeDtypeStruct((t, d), jnp.float32),
        grid=(t // tt,),
        in_specs=[pl.BlockSpec((1, 1, tt * nj), lambda i: (i, 0, 0), memory_space=pltpu.SMEM),
                  pl.BlockSpec((1, nj, tt), lambda i: (i, 0, 0)),
                  pl.BlockSpec((tt, d), lambda i: (i, 0)),
                  pl.BlockSpec((tt, d), lambda i: (i, 0)),
                  pl.BlockSpec(memory_space=pl.ANY),
                  pl.BlockSpec(memory_space=pl.ANY)],
        out_specs=pl.BlockSpec((tt, d), lambda i: (i, 0)),
        scratch_shapes=[pltpu.VMEM((tt * nj, d // LANES, LANES), jnp.float32),
                        pltpu.VMEM((tt * nj, d // LANES, LANES), jnp.float32),
                        pltpu.SemaphoreType.DMA((2,))],
        compiler_params=_cparams("arbitrary"),
        name="peer_experts",
    )(idx3, gate3, xn, h, u3, v3)


def _final_norm_kernel(x_ref, g_ref, o_ref):
    o_ref[...] = _rms(x_ref[...], g_ref[...])


def _final_norm(h, g, *, tm):
    t, d = h.shape
    tm = min(tm, t)
    return pl.pallas_call(
        _final_norm_kernel,
        out_shape=jax.ShapeDtypeStruct((t, d), jnp.float32),
        grid=(t // tm,),
        in_specs=[pl.BlockSpec((tm, d), lambda i: (i, 0)), pl.BlockSpec((1, d), lambda i: (0, 0))],
        out_specs=pl.BlockSpec((tm, d), lambda i: (i, 0)),
        compiler_params=_cparams("parallel"),
        name="final_norm",
    )(h, g.reshape(1, d))


def _rope_lane_tables(positions, rot_dim, period, lane0):
    half = rot_dim // 2
    inv = 1.0 / (ROPE_THETA ** (jnp.arange(0, rot_dim, 2, dtype=jnp.float32) / rot_dim))
    ang = positions.astype(jnp.float32).reshape(-1, 1) * inv[None, :]
    cos, sin = jnp.cos(ang), jnp.sin(ang)
    lane = np.arange(LANES) % period - lane0
    in_rot = (lane >= 0) & (lane < rot_dim)
    sel = np.where(in_rot, lane % half, 0)
    sign = np.where(lane < half, -1.0, 1.0).astype(np.float32)
    c = jnp.where(in_rot[None, :], cos[:, sel], 1.0)
    s = jnp.where(in_rot[None, :], sin[:, sel] * sign[None, :], 0.0)
    return c, s


def _layer_weights(l, w_in, mla_w_uq, mla_w_ukv):
    bf = jnp.bfloat16
    d = w_in.shape[1]
    offs = np.cumsum([0, CONV_CH, CONV_CH, CONV_CH, 512, 512, 512, MLA_Q_RANK, MLA_KV_RANK, MLA_ROPE])
    w = w_in[l]
    seg = lambda i: w[:, int(offs[i]):int(offs[i + 1])]
    gates = w[:, int(offs[9]):]
    kr = jnp.pad(seg(8), ((0, 0), (0, LANES - MLA_ROPE)))
    w_in_p = jnp.concatenate([gates, seg(0), seg(1), seg(2), seg(3), seg(4), seg(5), seg(6), seg(7), kr],
                             axis=1).astype(bf)
    qd = MLA_NOPE + MLA_ROPE
    wuq = mla_w_uq[l].reshape(MLA_Q_RANK, MLA_HEADS, qd)
    wuq = jnp.pad(wuq, ((0, 0), (0, 0), (0, HEAD_W - qd))).reshape(MLA_Q_RANK, MLA_HEADS * HEAD_W).astype(bf)
    wukv = mla_w_ukv[l].reshape(MLA_KV_RANK, MLA_HEADS, MLA_NOPE + MLA_V)
    wk = jnp.pad(wukv[:, :, :MLA_NOPE], ((0, 0), (0, 0), (0, HEAD_W - MLA_NOPE)))
    wk = wk.reshape(MLA_KV_RANK, MLA_HEADS * HEAD_W).astype(bf)
    wv = wukv[:, :, MLA_NOPE:].reshape(MLA_KV_RANK, MLA_HEADS * MLA_V).astype(bf)
    return w_in_p, wuq, wk, wv


def _placement():
    e = np.zeros((LANES, MLA_HEADS * HEAD_W), np.float32)
    for h in range(MLA_HEADS):
        for c in range(MLA_ROPE):
            e[c, h * HEAD_W + MLA_NOPE + c] = 1.0
    return jnp.asarray(e, jnp.bfloat16)


def kernel(x, mem, positions, norm_mix, w_in, conv_w, diff_lambda, diff_subln, mla_q_norm, mla_w_uq,
           mla_kv_norm, mla_w_ukv, w_branch, w_out, norm_cross, norm_mem, w_cq, w_ckv, w_co, norm_ffn,
           peer_w_q, peer_keys, peer_u, peer_v, final_norm):
    bf = jnp.bfloat16
    batch, seq, d = x.shape
    mem_tokens = mem.shape[1]
    depth = w_in.shape[0]
    t = batch * seq
    h = x.reshape(t, d)
    mem2 = mem.reshape(batch * mem_tokens, d)

    cd, sd = _rope_lane_tables(positions, DIFF_ROT, DIFF_HEAD_DIM, 0)
    cm, sm = _rope_lane_tables(positions, MLA_ROPE, LANES, MLA_NOPE)
    place = _placement()
    nj = PEER_HEADS * PEER_TOPK
    tt = 8

    for l in range(depth):
        lam_init = 0.8 - 0.6 * math.exp(-0.3 * l)
        w_in_p, wuq, wk, wv = _layer_weights(l, w_in, mla_w_uq, mla_w_ukv)

        proj = _norm_matmul(h, norm_mix[l], w_in_p, tm=512, tn=512, name="in_proj")
        dq, dk, mq, mk, mv = _prep(proj, cd, sd, cm, sm, mla_q_norm[l].reshape(1, -1), wuq,
                                   mla_kv_norm[l].reshape(1, -1), wk, wv, place, tm=512)
        y_b = _flash(dq, dk, proj, OFF_DV, batch, seq, DIFF_HEADS, n_maps=2, tq=512, tk=512,
                     lam_init=lam_init, lamv=diff_lambda[l], subln=diff_subln[l].reshape(1, -1),
                     name="diff_attn")
        y_c = _flash(mq, mk, mv, 0, batch, seq, MLA_HEADS, n_maps=1, tq=512, tk=512, name="mla_attn")
        h = _merge(proj, y_b, y_c, h, conv_w[l], w_branch[l].astype(bf), w_out[l].astype(bf), seq, tm=256)

        memkv = _norm_matmul(mem2, norm_mem[l], w_ckv[l].astype(bf), tm=256, tn=512, name="mem_kv")
        h = _cross(h, norm_cross[l].reshape(1, d), w_cq[l].astype(bf), memkv, w_co[l].astype(bf),
                   seq, mem_tokens, tm=256)

        xn, idx, gate = _route(h, norm_ffn[l].reshape(1, d), peer_w_q[l].astype(bf),
                               peer_keys[l].astype(bf), tm=256)
        idx3 = idx.T.reshape(t // tt, 1, tt * nj)
        gate3 = gate.reshape(nj, t // tt, tt).transpose(1, 0, 2)
        u3 = peer_u[l].reshape(-1, d // LANES, LANES)
        v3 = peer_v[l].reshape(-1, d // LANES, LANES)
        h = _experts(idx3, gate3, xn, h, u3, v3, tt=tt)

    out = _final_norm(h, final_norm, tm=512)
    return out.reshape(batch, seq, d)
```

```python
import functools
import math

import numpy as np
import jax
import jax.numpy as jnp
from jax import lax
from jax.experimental import pallas as pl
from jax.experimental.pallas import tpu as pltpu

EPS = 1e-6
ROPE_THETA = 500000.0
CONV_CH = 512
DIFF_HEADS = 4
DIFF_HEAD_DIM = 64
DIFF_ROT = DIFF_HEAD_DIM // 4
MLA_HEADS = 4
MLA_Q_RANK = 256
MLA_KV_RANK = 128
MLA_NOPE = 64
MLA_ROPE = 32
MLA_V = 128
N_BRANCH = 3
CROSS_HEADS = 4
PEER_HEADS = 8
PEER_N_KEYS = 128
PEER_KEY_DIM = 128
PEER_TOPK = 16

LANES = 128
SUBLANES = 8
VMEM_LIMIT_BYTES = 56 * 1024 * 1024

HEAD_W = 128
OFF_GATES = 0
OFF_AB = 3072
OFF_AC = OFF_AB + CONV_CH
OFF_AH = OFF_AC + CONV_CH
OFF_DQ = OFF_AH + CONV_CH
OFF_DK = OFF_DQ + 512
OFF_DV = OFF_DK + 512
OFF_CQ = OFF_DV + 512
OFF_CKV = OFF_CQ + MLA_Q_RANK
OFF_KR = OFF_CKV + MLA_KV_RANK
PROJ_W = OFF_KR + LANES


def _cparams(*sem):
    return pltpu.CompilerParams(dimension_semantics=sem, vmem_limit_bytes=VMEM_LIMIT_BYTES)


def _norm_matmul_kernel(x_ref, g_ref, w_ref, o_ref, xn_ref):
    @pl.when(pl.program_id(1) == 0)
    def _():
        x = x_ref[...]
        ms = jnp.mean(x * x, axis=-1, keepdims=True)
        xn_ref[...] = (x * lax.rsqrt(ms + EPS) * g_ref[...]).astype(xn_ref.dtype)

    o_ref[...] = jnp.dot(xn_ref[...], w_ref[...],
                         preferred_element_type=jnp.float32).astype(o_ref.dtype)


def _norm_matmul(x, g, w, *, tm, tn, name):
    t, k = x.shape
    n = w.shape[1]
    tm = min(tm, t)
    return pl.pallas_call(
        _norm_matmul_kernel,
        out_shape=jax.ShapeDtypeStruct((t, n), jnp.bfloat16),
        grid=(t // tm, n // tn),
        in_specs=[pl.BlockSpec((tm, k), lambda i, j: (i, 0)),
                  pl.BlockSpec((1, k), lambda i, j: (0, 0)),
                  pl.BlockSpec((k, tn), lambda i, j: (0, j))],
        out_specs=pl.BlockSpec((tm, tn), lambda i, j: (i, j)),
        scratch_shapes=[pltpu.VMEM((tm, k), jnp.bfloat16)],
        compiler_params=_cparams("parallel", "arbitrary"),
        name=name,
    )(x, g.reshape(1, k), w)


def _rope_block(xb, c, s, lo_mask, shift):
    partner = jnp.where(lo_mask, pltpu.roll(xb, LANES - shift, axis=1), pltpu.roll(xb, shift, axis=1))
    return xb * c + partner * s


def _rms(x, g):
    ms = jnp.mean(x * x, axis=-1, keepdims=True)
    return x * lax.rsqrt(ms + EPS) * g


def _prep_kernel(dq_ref, dk_ref, cq_ref, ckv_ref, kr_ref, cd_ref, sd_ref, cm_ref, sm_ref,
                 qn_ref, wuq_ref, kvn_ref, wk_ref, wv_ref, place_ref,
                 dq_o, dk_o, mq_o, mk_o, mv_o):
    f32 = jnp.float32
    tm = dq_ref.shape[0]
    lane = lax.broadcasted_iota(jnp.int32, (tm, LANES), 1)
    cd, sd, cm, sm = cd_ref[...], sd_ref[...], cm_ref[...], sm_ref[...]
    d_lo = (lane % DIFF_HEAD_DIM) < (DIFF_ROT // 2)
    m_lo = lane < (MLA_NOPE + MLA_ROPE // 2)
    d_scale = DIFF_HEAD_DIM ** -0.5
    m_scale = (MLA_NOPE + MLA_ROPE) ** -0.5
    first_half = lane < DIFF_HEAD_DIM

    for h in range(DIFF_HEADS):
        sl = slice(h * HEAD_W, (h + 1) * HEAD_W)
        qb = _rope_block(dq_ref[:, sl].astype(f32), cd, sd, d_lo, DIFF_ROT // 2) * d_scale
        kb = _rope_block(dk_ref[:, sl].astype(f32), cd, sd, d_lo, DIFF_ROT // 2)
        dq_o[:, 2 * h * HEAD_W:(2 * h + 1) * HEAD_W] = jnp.where(first_half, qb, 0.0).astype(dq_o.dtype)
        dq_o[:, (2 * h + 1) * HEAD_W:(2 * h + 2) * HEAD_W] = jnp.where(first_half, 0.0, qb).astype(dq_o.dtype)
        dk_o[:, sl] = kb.astype(dk_o.dtype)

    cqn = _rms(cq_ref[...].astype(f32), qn_ref[...]).astype(jnp.bfloat16)
    q = jnp.dot(cqn, wuq_ref[...], preferred_element_type=f32)
    ckvn = _rms(ckv_ref[...].astype(f32), kvn_ref[...]).astype(jnp.bfloat16)
    k = jnp.dot(ckvn, wk_ref[...], preferred_element_type=f32)
    k = k + jnp.dot(kr_ref[...], place_ref[...], preferred_element_type=f32)
    v = jnp.dot(ckvn, wv_ref[...], preferred_element_type=f32)
    for h in range(MLA_HEADS):
        sl = slice(h * HEAD_W, (h + 1) * HEAD_W)
        mq_o[:, sl] = (_rope_block(q[:, sl], cm, sm, m_lo, MLA_ROPE // 2) * m_scale).astype(mq_o.dtype)
        mk_o[:, sl] = _rope_block(k[:, sl], cm, sm, m_lo, MLA_ROPE // 2).astype(mk_o.dtype)
    mv_o[...] = v.astype(mv_o.dtype)


def _prep(proj, cd, sd, cm, sm, qn, wuq, kvn, wk, wv, place, *, tm):
    t = proj.shape[0]
    tm = min(tm, t)
    row = lambda w, off: pl.BlockSpec((tm, w), lambda i, _o=off // w: (i, _o))
    full = lambda a: pl.BlockSpec(a.shape, lambda i: (0,) * a.ndim)
    tab = pl.BlockSpec((tm, LANES), lambda i: (i, 0))
    bf = jnp.bfloat16
    return pl.pallas_call(
        _prep_kernel,
        out_shape=[jax.ShapeDtypeStruct((t, 2 * DIFF_HEADS * HEAD_W), bf),
                   jax.ShapeDtypeStruct((t, DIFF_HEADS * HEAD_W), bf),
                   jax.ShapeDtypeStruct((t, MLA_HEADS * HEAD_W), bf),
                   jax.ShapeDtypeStruct((t, MLA_HEADS * HEAD_W), bf),
                   jax.ShapeDtypeStruct((t, MLA_HEADS * HEAD_W), bf)],
        grid=(t // tm,),
        in_specs=[row(512, OFF_DQ), row(512, OFF_DK), row(MLA_Q_RANK, OFF_CQ),
                  row(MLA_KV_RANK, OFF_CKV), row(LANES, OFF_KR), tab, tab, tab, tab,
                  full(qn), full(wuq), full(kvn), full(wk), full(wv), full(place)],
        out_specs=[pl.BlockSpec((tm, 2 * DIFF_HEADS * HEAD_W), lambda i: (i, 0)),
                   pl.BlockSpec((tm, DIFF_HEADS * HEAD_W), lambda i: (i, 0)),
                   pl.BlockSpec((tm, MLA_HEADS * HEAD_W), lambda i: (i, 0)),
                   pl.BlockSpec((tm, MLA_HEADS * HEAD_W), lambda i: (i, 0)),
                   pl.BlockSpec((tm, MLA_HEADS * HEAD_W), lambda i: (i, 0))],
        compiler_params=_cparams("parallel"),
        name="attn_prep",
    )(proj, proj, proj, proj, proj, cd, sd, cm, sm, qn, wuq, kvn, wk, wv, place)


def _flash_kernel(*refs, n_maps, tk, lam_init):
    if n_maps == 2:
        q_ref, k_ref, v_ref, lamv_ref, subln_ref, o_ref, m_ref, l_ref, acc_ref = refs
    else:
        q_ref, k_ref, v_ref, o_ref, m_ref, l_ref, acc_ref = refs
    f32 = jnp.float32
    tq = q_ref.shape[0]
    nk = k_ref.shape[0] // tk
    m_ref[...] = jnp.full(m_ref.shape, -jnp.inf, f32)
    l_ref[...] = jnp.zeros(l_ref.shape, f32)
    acc_ref[...] = jnp.zeros(acc_ref.shape, f32)

    def body(c, carry):
        start = pl.multiple_of(c * tk, tk)
        kc = k_ref[pl.ds(start, tk), :]
        vc = v_ref[pl.ds(start, tk), :]
        for mi in range(n_maps):
            q = q_ref[:, mi * HEAD_W:(mi + 1) * HEAD_W]
            s = lax.dot_general(q, kc, (((1,), (1,)), ((), ())), preferred_element_type=f32)
            m_prev = m_ref[mi]
            m_new = jnp.maximum(m_prev, jnp.max(s, axis=-1, keepdims=True))
            alpha = jnp.exp(m_prev - m_new)
            p = jnp.exp(s - m_new[:, :1])
            l_ref[mi] = alpha * l_ref[mi] + jnp.sum(p, axis=-1, keepdims=True)
            acc_ref[mi] = alpha * acc_ref[mi] + jnp.dot(p.astype(vc.dtype), vc, preferred_element_type=f32)
            m_ref[mi] = m_new
        return carry

    lax.fori_loop(0, nk, body, 0)

    if n_maps == 2:
        lv = lamv_ref[...]
        lam = (jnp.exp(jnp.sum(lv[0:1] * lv[1:2], axis=-1, keepdims=True))
               - jnp.exp(jnp.sum(lv[2:3] * lv[3:4], axis=-1, keepdims=True)) + lam_init)
        o = acc_ref[0] / l_ref[0] - lam * (acc_ref[1] / l_ref[1])
        o = _rms(o, subln_ref[...]) * (1.0 - lam_init)
    else:
        o = acc_ref[0] / l_ref[0]
    o_ref[...] = o.astype(o_ref.dtype)


def _flash(q, k, v, v_col0, batch, seq, heads, *, n_maps, tq, tk, lam_init=0.0, lamv=None, subln=None, name):
    tq = min(tq, seq)
    tk = min(tk, seq)
    nq = seq // tq
    vb = v_col0 // HEAD_W
    in_specs = [pl.BlockSpec((tq, n_maps * HEAD_W), lambda b, h, i: (b * nq + i, h)),
                pl.BlockSpec((seq, HEAD_W), lambda b, h, i: (b, h)),
                pl.BlockSpec((seq, HEAD_W), lambda b, h, i: (b, vb + h))]
    args = [q, k, v]
    if n_maps == 2:
        in_specs += [pl.BlockSpec(lamv.shape, lambda b, h, i: (0, 0)),
                     pl.BlockSpec(subln.shape, lambda b, h, i: (0, 0))]
        args += [lamv, subln]
    return pl.pallas_call(
        functools.partial(_flash_kernel, n_maps=n_maps, tk=tk, lam_init=lam_init),
        out_shape=jax.ShapeDtypeStruct((batch * seq, heads * HEAD_W), jnp.bfloat16),
        grid=(batch, heads, nq),
        in_specs=in_specs,
        out_specs=pl.BlockSpec((tq, HEAD_W), lambda b, h, i: (b * nq + i, h)),
        scratch_shapes=[pltpu.VMEM((n_maps, tq, HEAD_W), jnp.float32),
                        pltpu.VMEM((n_maps, tq, HEAD_W), jnp.float32),
                        pltpu.VMEM((n_maps, tq, HEAD_W), jnp.float32)],
        compiler_params=_cparams("parallel", "parallel", "arbitrary"),
        name=name,
    )(*args)


def _sigmoid(x):
    return 1.0 / (1.0 + jnp.exp(-x))


def _merge_kernel(g0_ref, g1_ref, g2_ref, ab_ref, ac_ref, ah_ref, pc_ref, ph_ref, nc_ref, nh_ref,
                  yb_ref, yc_ref, h_ref, cw_ref, wb_ref, wo_ref, o_ref, *, tiles_per_seq):
    f32 = jnp.float32
    tm = ab_ref.shape[0]
    i = pl.program_id(0)
    pos = i % tiles_per_seq
    z = ac_ref[...].astype(f32) * ah_ref[...].astype(f32)
    zp_edge = pc_ref[SUBLANES - 1:SUBLANES, :].astype(f32) * ph_ref[SUBLANES - 1:SUBLANES, :].astype(f32)
    zn_edge = nc_ref[0:1, :].astype(f32) * nh_ref[0:1, :].astype(f32)
    zp_edge = jnp.where(pos > 0, zp_edge, 0.0)
    zn_edge = jnp.where(pos < tiles_per_seq - 1, zn_edge, 0.0)
    row = lax.broadcasted_iota(jnp.int32, z.shape, 0)
    zp = jnp.where(row == 0, zp_edge, pltpu.roll(z, 1, axis=0))
    zn = jnp.where(row == tm - 1, zn_edge, pltpu.roll(z, tm - 1, axis=0))
    cw = cw_ref[...]
    y_a = ab_ref[...].astype(f32) * (cw[0:1] * zp + cw[1:2] * z + cw[2:3] * zn)

    bf = jnp.bfloat16
    merged = _sigmoid(g0_ref[...].astype(f32)) * jnp.dot(y_a.astype(bf), wb_ref[0], preferred_element_type=f32)
    merged += _sigmoid(g1_ref[...].astype(f32)) * jnp.dot(yb_ref[...], wb_ref[1], preferred_element_type=f32)
    merged += _sigmoid(g2_ref[...].astype(f32)) * jnp.dot(yc_ref[...], wb_ref[2], preferred_element_type=f32)
    o_ref[...] = h_ref[...] + jnp.dot(merged.astype(bf), wo_ref[...], preferred_element_type=f32)


def _merge(proj, y_b, y_c, h, conv_w, w_branch, w_out, seq, *, tm):
    t, d = h.shape
    tm = min(tm, seq)
    tiles_per_seq = seq // tm
    r8 = tm // SUBLANES
    nrow8 = t // SUBLANES
    row = lambda w, off: pl.BlockSpec((tm, w), lambda i, _o=off // w: (i, _o))
    prev = lambda off: pl.BlockSpec((SUBLANES, CONV_CH),
                                    lambda i, _o=off // CONV_CH: (jnp.maximum(i * r8 - 1, 0), _o))
    nxt = lambda off: pl.BlockSpec((SUBLANES, CONV_CH),
                                   lambda i, _o=off // CONV_CH: (jnp.minimum((i + 1) * r8, nrow8 - 1), _o))
    full = lambda a: pl.BlockSpec(a.shape, lambda i: (0,) * a.ndim)
    return pl.pallas_call(
        functools.partial(_merge_kernel, tiles_per_seq=tiles_per_seq),
        out_shape=jax.ShapeDtypeStruct((t, d), jnp.float32),
        grid=(t // tm,),
        in_specs=[row(d, OFF_GATES), row(d, OFF_GATES + d), row(d, OFF_GATES + 2 * d),
                  row(CONV_CH, OFF_AB), row(CONV_CH, OFF_AC), row(CONV_CH, OFF_AH),
                  prev(OFF_AC), prev(OFF_AH), nxt(OFF_AC), nxt(OFF_AH),
                  pl.BlockSpec((tm, y_b.shape[1]), lambda i: (i, 0)),
                  pl.BlockSpec((tm, y_c.shape[1]), lambda i: (i, 0)),
                  pl.BlockSpec((tm, d), lambda i: (i, 0)),
                  full(conv_w), full(w_branch), full(w_out)],
        out_specs=pl.BlockSpec((tm, d), lambda i: (i, 0)),
        compiler_params=_cparams("parallel"),
        name="gated_merge",
    )(proj, proj, proj, proj, proj, proj, proj, proj, proj, proj, y_b, y_c, h, conv_w, w_branch, w_out)


def _cross_kernel(h_ref, g_ref, wq_ref, kv_ref, wo_ref, o_ref):
    f32, bf = jnp.float32, jnp.bfloat16
    h = h_ref[...]
    d = h.shape[1]
    hd = d // CROSS_HEADS
    hn = _rms(h, g_ref[...]).astype(bf)
    q = (jnp.dot(hn, wq_ref[...], preferred_element_type=f32) * (hd ** -0.5)).astype(bf)
    outs = []
    for hh in range(CROSS_HEADS):
        kh = kv_ref[:, hh * hd:(hh + 1) * hd]
        vh = kv_ref[:, d + hh * hd:d + (hh + 1) * hd]
        s = lax.dot_general(q[:, hh * hd:(hh + 1) * hd], kh, (((1,), (1,)), ((), ())),
                            preferred_element_type=f32)
        p = jnp.exp(s - jnp.max(s, axis=-1, keepdims=True))
        p = p / jnp.sum(p, axis=-1, keepdims=True)
        outs.append(jnp.dot(p.astype(bf), vh, preferred_element_type=f32).astype(bf))
    o = jnp.concatenate(outs, axis=-1)
    o_ref[...] = h + jnp.dot(o, wo_ref[...], preferred_element_type=f32)


def _cross(h, g, w_cq, memkv, w_co, seq, mem_tokens, *, tm):
    t, d = h.shape
    tm = min(tm, seq)
    tiles_per_seq = seq // tm
    full = lambda a: pl.BlockSpec(a.shape, lambda i: (0,) * a.ndim)
    return pl.pallas_call(
        _cross_kernel,
        out_shape=jax.ShapeDtypeStruct((t, d), jnp.float32),
        grid=(t // tm,),
        in_specs=[pl.BlockSpec((tm, d), lambda i: (i, 0)), full(g), full(w_cq),
                  pl.BlockSpec((mem_tokens, 2 * d), lambda i: (i // tiles_per_seq, 0)), full(w_co)],
        out_specs=pl.BlockSpec((tm, d), lambda i: (i, 0)),
        compiler_params=_cparams("parallel"),
        name="cross_attn",
    )(h, g, w_cq, memkv, w_co)


def _topk_rows(sc, k, vals_ref, idx_ref, payload=None):
    n = sc.shape[0]
    row = lax.broadcasted_iota(jnp.int32, sc.shape, 0).astype(jnp.float32)
    for r in range(k):
        m = jnp.max(sc, axis=0, keepdims=True)
        first = jnp.min(jnp.where(sc == m, row, float(n)), axis=0, keepdims=True)
        hit = row == first
        vals_ref[r:r + 1, :] = m
        if payload is None:
            idx_ref[r:r + 1, :] = first
        else:
            idx_ref[r:r + 1, :] = jnp.sum(jnp.where(hit, payload, 0.0), axis=0, keepdims=True)
        sc = jnp.where(hit, -jnp.inf, sc)


def _route_kernel(h_ref, g_ref, wq_ref, keys_ref, xn_o, idx_o, gate_o,
                  s1_ref, i1_ref, s2_ref, i2_ref, cs_ref, ci_ref, ts_ref, ti_ref):
    f32, bf = jnp.float32, jnp.bfloat16
    xn = _rms(h_ref[...], g_ref[...])
    xn_o[...] = xn
    q = jnp.dot(xn.astype(bf), wq_ref[...], preferred_element_type=f32).astype(bf)
    kd = PEER_KEY_DIM
    for hh in range(PEER_HEADS):
        for p, (s_ref, i_ref) in enumerate(((s1_ref, i1_ref), (s2_ref, i2_ref))):
            c0 = (hh * 2 + p) * kd
            sc = lax.dot_general(keys_ref[hh, p], q[:, c0:c0 + kd], (((1,), (1,)), ((), ())),
                                 preferred_element_type=f32)
            _topk_rows(sc, PEER_TOPK, s_ref, i_ref)
        s2 = s2_ref[...]
        i2 = i2_ref[...]
        for a in range(PEER_TOPK):
            cs_ref[a * PEER_TOPK:(a + 1) * PEER_TOPK, :] = s1_ref[a:a + 1, :] + s2
            ci_ref[a * PEER_TOPK:(a + 1) * PEER_TOPK, :] = i1_ref[a:a + 1, :] * float(PEER_N_KEYS) + i2
        _topk_rows(cs_ref[...], PEER_TOPK, ts_ref, ti_ref, payload=ci_ref[...])
        ts = ts_ref[...]
        e = jnp.exp(ts - ts[0:1, :])
        gate_o[hh * PEER_TOPK:(hh + 1) * PEER_TOPK, :] = e / jnp.sum(e, axis=0, keepdims=True)
        idx_o[hh * PEER_TOPK:(hh + 1) * PEER_TOPK, :] = ti_ref[...].astype(jnp.int32)


def _route(h, g, w_q, keys, *, tm):
    t, d = h.shape
    tm = min(tm, t)
    nj = PEER_HEADS * PEER_TOPK
    full = lambda a: pl.BlockSpec(a.shape, lambda i: (0,) * a.ndim)
    small = lambda: pltpu.VMEM((PEER_TOPK, tm), jnp.float32)
    big = lambda: pltpu.VMEM((PEER_TOPK * PEER_TOPK, tm), jnp.float32)
    return pl.pallas_call(
        _route_kernel,
        out_shape=[jax.ShapeDtypeStruct((t, d), jnp.float32),
                   jax.ShapeDtypeStruct((nj, t), jnp.int32),
                   jax.ShapeDtypeStruct((nj, t), jnp.float32)],
        grid=(t // tm,),
        in_specs=[pl.BlockSpec((tm, d), lambda i: (i, 0)), full(g), full(w_q), full(keys)],
        out_specs=[pl.BlockSpec((tm, d), lambda i: (i, 0)),
                   pl.BlockSpec((nj, tm), lambda i: (0, i)),
                   pl.BlockSpec((nj, tm), lambda i: (0, i))],
        scratch_shapes=[small(), small(), small(), small(), big(), big(), small(), small()],
        compiler_params=_cparams("parallel"),
        name="peer_route",
    )(h, g, w_q, keys)


def _gelu(x):
    return 0.5 * x * (1.0 + lax.erf(x * (2.0 ** -0.5)))


def _experts_kernel(idx_ref, idxn_ref, gate_ref, xn_ref, h_ref, uv_hbm, o_ref, buf, sem):
    f32, bf = jnp.float32, jnp.bfloat16
    tt, d = xn_ref.shape
    nj = gate_ref.shape[1]
    nrows = tt * nj
    nchunk = d // LANES
    ngroups = nrows // SUBLANES
    i = pl.program_id(0)
    n = pl.num_programs(0)
    slot = lax.rem(i, 2)

    def issue(iref, sl):
        def body(g, carry):
            for k in range(SUBLANES):
                e = iref[0, 0, g * SUBLANES + k]
                pltpu.make_async_copy(uv_hbm.at[e], buf.at[sl, g, :, k, :], sem.at[sl]).start(priority=k % 2)
            return carry
        lax.fori_loop(0, ngroups, body, 0, unroll=2)

    @pl.when(i == 0)
    def _():
        issue(idx_ref, slot)

    @pl.when(i + 1 < n)
    def _():
        issue(idxn_ref, 1 - slot)

    pltpu.make_async_copy(buf.at[slot], buf.at[slot], sem.at[slot]).wait()
    xn = xn_ref[...].astype(bf)
    act = jnp.zeros((tt, nrows), f32)
    for c in range(nchunk):
        uc = buf[slot, :, c, :, :].reshape(nrows, LANES).astype(bf)
        act += lax.dot_general(xn[:, c * LANES:(c + 1) * LANES], uc, (((1,), (1,)), ((), ())),
                               preferred_element_type=f32)
    col = lax.broadcasted_iota(jnp.int32, (tt, nrows), 1)
    row = lax.broadcasted_iota(jnp.int32, (tt, nrows), 0)
    own = (col >= row * nj) & (col < (row + 1) * nj)
    gate = jnp.concatenate([gate_ref[...]] * tt, axis=1)
    w = jnp.where(own, gate * _gelu(act), 0.0).astype(bf)
    for c in range(nchunk):
        vc = buf[slot, :, nchunk + c, :, :].reshape(nrows, LANES).astype(bf)
        sl = slice(c * LANES, (c + 1) * LANES)
        o_ref[:, sl] = h_ref[:, sl] + jnp.dot(w, vc, preferred_element_type=f32)


def _experts(idx3, gate, xn, h, uv, *, tt):
    t, d = h.shape
    nj = gate.shape[1]
    n = t // tt
    return pl.pallas_call(
        _experts_kernel,
        out_shape=jax.ShapeDtypeStruct((t, d), jnp.float32),
        grid=(n,),
        in_specs=[pl.BlockSpec((1, 1, tt * nj), lambda i: (i, 0, 0), memory_space=pltpu.SMEM),
                  pl.BlockSpec((1, 1, tt * nj), lambda i: (jnp.minimum(i + 1, n - 1), 0, 0),
                               memory_space=pltpu.SMEM),
                  pl.BlockSpec((tt, nj), lambda i: (i, 0)),
                  pl.BlockSpec((tt, d), lambda i: (i, 0)),
                  pl.BlockSpec((tt, d), lambda i: (i, 0)),
                  pl.BlockSpec(memory_space=pl.ANY)],
        out_specs=pl.BlockSpec((tt, d), lambda i: (i, 0)),
        scratch_shapes=[pltpu.VMEM((2, tt * nj // SUBLANES, 2 * d // LANES, SUBLANES, LANES), jnp.float32),
                        pltpu.SemaphoreType.DMA((2,))],
        compiler_params=_cparams("arbitrary"),
        name="peer_experts",
    )(idx3, idx3, gate, xn, h, uv)


def _final_norm_kernel(x_ref, g_ref, o_ref):
    o_ref[...] = _rms(x_ref[...], g_ref[...])


def _final_norm(h, g, *, tm):
    t, d = h.shape
    tm = min(tm, t)
    return pl.pallas_call(
        _final_norm_kernel,
        out_shape=jax.ShapeDtypeStruct((t, d), jnp.float32),
        grid=(t // tm,),
        in_specs=[pl.BlockSpec((tm, d), lambda i: (i, 0)), pl.BlockSpec((1, d), lambda i: (0, 0))],
        out_specs=pl.BlockSpec((tm, d), lambda i: (i, 0)),
        compiler_params=_cparams("parallel"),
        name="final_norm",
    )(h, g.reshape(1, d))


def _rope_lane_tables(positions, rot_dim, period, lane0):
    half = rot_dim // 2
    inv = 1.0 / (ROPE_THETA ** (jnp.arange(0, rot_dim, 2, dtype=jnp.float32) / rot_dim))
    ang = positions.astype(jnp.float32).reshape(-1, 1) * inv[None, :]
    cos, sin = jnp.cos(ang), jnp.sin(ang)
    lane = np.arange(LANES) % period - lane0
    in_rot = (lane >= 0) & (lane < rot_dim)
    sel = np.where(in_rot, lane % half, 0)
    sign = np.where(lane < half, -1.0, 1.0).astype(np.float32)
    c = jnp.where(in_rot[None, :], cos[:, sel], 1.0)
    s = jnp.where(in_rot[None, :], sin[:, sel] * sign[None, :], 0.0)
    return c, s


def _layer_weights(l, w_in, mla_w_uq, mla_w_ukv):
    bf = jnp.bfloat16
    d = w_in.shape[1]
    offs = np.cumsum([0, CONV_CH, CONV_CH, CONV_CH, 512, 512, 512, MLA_Q_RANK, MLA_KV_RANK, MLA_ROPE])
    w = w_in[l]
    seg = lambda i: w[:, int(offs[i]):int(offs[i + 1])]
    gates = w[:, int(offs[9]):]
    kr = jnp.pad(seg(8), ((0, 0), (0, LANES - MLA_ROPE)))
    w_in_p = jnp.concatenate([gates, seg(0), seg(1), seg(2), seg(3), seg(4), seg(5), seg(6), seg(7), kr],
                             axis=1).astype(bf)
    qd = MLA_NOPE + MLA_ROPE
    wuq = mla_w_uq[l].reshape(MLA_Q_RANK, MLA_HEADS, qd)
    wuq = jnp.pad(wuq, ((0, 0), (0, 0), (0, HEAD_W - qd))).reshape(MLA_Q_RANK, MLA_HEADS * HEAD_W).astype(bf)
    wukv = mla_w_ukv[l].reshape(MLA_KV_RANK, MLA_HEADS, MLA_NOPE + MLA_V)
    wk = jnp.pad(wukv[:, :, :MLA_NOPE], ((0, 0), (0, 0), (0, HEAD_W - MLA_NOPE)))
    wk = wk.reshape(MLA_KV_RANK, MLA_HEADS * HEAD_W).astype(bf)
    wv = wukv[:, :, MLA_NOPE:].reshape(MLA_KV_RANK, MLA_HEADS * MLA_V).astype(bf)
    return w_in_p, wuq, wk, wv


def _placement():
    e = np.zeros((LANES, MLA_HEADS * HEAD_W), np.float32)
    for h in range(MLA_HEADS):
        for c in range(MLA_ROPE):
            e[c, h * HEAD_W + MLA_NOPE + c] = 1.0
    return jnp.asarray(e, jnp.bfloat16)


def kernel(x, mem, positions, norm_mix, w_in, conv_w, diff_lambda, diff_subln, mla_q_norm, mla_w_uq,
           mla_kv_norm, mla_w_ukv, w_branch, w_out, norm_cross, norm_mem, w_cq, w_ckv, w_co, norm_ffn,
           peer_w_q, peer_keys, peer_u, peer_v, final_norm):
    bf = jnp.bfloat16
    batch, seq, d = x.shape
    mem_tokens = mem.shape[1]
    depth = w_in.shape[0]
    t = batch * seq
    h = x.reshape(t, d)
    mem2 = mem.reshape(batch * mem_tokens, d)

    cd, sd = _rope_lane_tables(positions, DIFF_ROT, DIFF_HEAD_DIM, 0)
    cm, sm = _rope_lane_tables(positions, MLA_ROPE, LANES, MLA_NOPE)
    place = _placement()
    nj = PEER_HEADS * PEER_TOPK
    tt = 8

    for l in range(depth):
        lam_init = 0.8 - 0.6 * math.exp(-0.3 * l)
        w_in_p, wuq, wk, wv = _layer_weights(l, w_in, mla_w_uq, mla_w_ukv)

        proj = _norm_matmul(h, norm_mix[l], w_in_p, tm=512, tn=512, name="in_proj")
        dq, dk, mq, mk, mv = _prep(proj, cd, sd, cm, sm, mla_q_norm[l].reshape(1, -1), wuq,
                                   mla_kv_norm[l].reshape(1, -1), wk, wv, place, tm=512)
        y_b = _flash(dq, dk, proj, OFF_DV, batch, seq, DIFF_HEADS, n_maps=2, tq=512, tk=512,
                     lam_init=lam_init, lamv=diff_lambda[l], subln=diff_subln[l].reshape(1, -1),
                     name="diff_attn")
        y_c = _flash(mq, mk, mv, 0, batch, seq, MLA_HEADS, n_maps=1, tq=512, tk=512, name="mla_attn")
        h = _merge(proj, y_b, y_c, h, conv_w[l], w_branch[l].astype(bf), w_out[l].astype(bf), seq, tm=256)

        memkv = _norm_matmul(mem2, norm_mem[l], w_ckv[l].astype(bf), tm=256, tn=512, name="mem_kv")
        h = _cross(h, norm_cross[l].reshape(1, d), w_cq[l].astype(bf), memkv, w_co[l].astype(bf),
                   seq, mem_tokens, tm=256)

        xn, idx, gate = _route(h, norm_ffn[l].reshape(1, d), peer_w_q[l].astype(bf),
                               peer_keys[l].astype(bf), tm=256)
        idx3 = idx.T.reshape(t // tt, 1, tt * nj)
        uv = jnp.concatenate([peer_u[l].reshape(-1, d // LANES, LANES),
                              peer_v[l].reshape(-1, d // LANES, LANES)], axis=1)
        h = _experts(idx3, gate.T, xn, h, uv, tt=tt)

    out = _final_norm(h, final_norm, tm=512)
    return out.reshape(batch, seq, d)
```

```python
import functools
import math

import numpy as np
import jax
import jax.numpy as jnp
from jax import lax
from jax.experimental import pallas as pl
from jax.experimental.pallas import tpu as pltpu

EPS = 1e-6
ROPE_THETA = 500000.0
CONV_CH = 512
DIFF_HEADS = 4
DIFF_HEAD_DIM = 64
DIFF_ROT = DIFF_HEAD_DIM // 4
MLA_HEADS = 4
MLA_Q_RANK = 256
MLA_KV_RANK = 128
MLA_NOPE = 64
MLA_ROPE = 32
MLA_V = 128
N_BRANCH = 3
CROSS_HEADS = 4
PEER_HEADS = 8
PEER_N_KEYS = 128
PEER_KEY_DIM = 128
PEER_TOPK = 16

LANES = 128
SUBLANES = 8
VMEM_LIMIT_BYTES = 56 * 1024 * 1024

ATTN_TQ = 512
ATTN_TK = 8192

HEAD_W = 128
OFF_GATES = 0
OFF_AB = 3072
OFF_AC = OFF_AB + CONV_CH
OFF_AH = OFF_AC + CONV_CH
OFF_DQ = OFF_AH + CONV_CH
OFF_DK = OFF_DQ + 512
OFF_DV = OFF_DK + 512
OFF_CQ = OFF_DV + 512
OFF_CKV = OFF_CQ + MLA_Q_RANK
OFF_KR = OFF_CKV + MLA_KV_RANK
PROJ_W = OFF_KR + LANES


def _cparams(*sem):
    return pltpu.CompilerParams(dimension_semantics=sem, vmem_limit_bytes=VMEM_LIMIT_BYTES)


def _norm_matmul_kernel(x_ref, g_ref, w_ref, o_ref, xn_ref):
    @pl.when(pl.program_id(1) == 0)
    def _():
        x = x_ref[...]
        ms = jnp.mean(x * x, axis=-1, keepdims=True)
        xn_ref[...] = (x * lax.rsqrt(ms + EPS) * g_ref[...]).astype(xn_ref.dtype)

    o_ref[...] = jnp.dot(xn_ref[...], w_ref[...],
                         preferred_element_type=jnp.float32).astype(o_ref.dtype)


def _norm_matmul(x, g, w, *, tm, tn, name):
    t, k = x.shape
    n = w.shape[1]
    tm = min(tm, t)
    return pl.pallas_call(
        _norm_matmul_kernel,
        out_shape=jax.ShapeDtypeStruct((t, n), jnp.bfloat16),
        grid=(t // tm, n // tn),
        in_specs=[pl.BlockSpec((tm, k), lambda i, j: (i, 0)),
                  pl.BlockSpec((1, k), lambda i, j: (0, 0)),
                  pl.BlockSpec((k, tn), lambda i, j: (0, j))],
        out_specs=pl.BlockSpec((tm, tn), lambda i, j: (i, j)),
        scratch_shapes=[pltpu.VMEM((tm, k), jnp.bfloat16)],
        compiler_params=_cparams("parallel", "arbitrary"),
        name=name,
    )(x, g.reshape(1, k), w)


def _rope_block(xb, c, s, lo_mask, shift):
    partner = jnp.where(lo_mask, pltpu.roll(xb, LANES - shift, axis=1), pltpu.roll(xb, shift, axis=1))
    return xb * c + partner * s


def _rms(x, g):
    ms = jnp.mean(x * x, axis=-1, keepdims=True)
    return x * lax.rsqrt(ms + EPS) * g


def _prep_kernel(dq_ref, dk_ref, cq_ref, ckv_ref, kr_ref, cd_ref, sd_ref, cm_ref, sm_ref,
                 qn_ref, wuq_ref, kvn_ref, wk_ref, wv_ref, place_ref,
                 dq_o, dk_o, mq_o, mk_o, mv_o):
    f32 = jnp.float32
    tm = dq_ref.shape[0]
    lane = lax.broadcasted_iota(jnp.int32, (tm, LANES), 1)
    cd, sd, cm, sm = cd_ref[...], sd_ref[...], cm_ref[...], sm_ref[...]
    d_lo = (lane % DIFF_HEAD_DIM) < (DIFF_ROT // 2)
    m_lo = lane < (MLA_NOPE + MLA_ROPE // 2)
    d_scale = DIFF_HEAD_DIM ** -0.5
    m_scale = (MLA_NOPE + MLA_ROPE) ** -0.5
    first_half = lane < DIFF_HEAD_DIM

    for h in range(DIFF_HEADS):
        sl = slice(h * HEAD_W, (h + 1) * HEAD_W)
        qb = _rope_block(dq_ref[:, sl].astype(f32), cd, sd, d_lo, DIFF_ROT // 2) * d_scale
        kb = _rope_block(dk_ref[:, sl].astype(f32), cd, sd, d_lo, DIFF_ROT // 2)
        dq_o[:, 2 * h * HEAD_W:(2 * h + 1) * HEAD_W] = jnp.where(first_half, qb, 0.0).astype(dq_o.dtype)
        dq_o[:, (2 * h + 1) * HEAD_W:(2 * h + 2) * HEAD_W] = jnp.where(first_half, 0.0, qb).astype(dq_o.dtype)
        dk_o[:, sl] = kb.astype(dk_o.dtype)

    cqn = _rms(cq_ref[...].astype(f32), qn_ref[...]).astype(jnp.bfloat16)
    q = jnp.dot(cqn, wuq_ref[...], preferred_element_type=f32)
    ckvn = _rms(ckv_ref[...].astype(f32), kvn_ref[...]).astype(jnp.bfloat16)
    k = jnp.dot(ckvn, wk_ref[...], preferred_element_type=f32)
    k = k + jnp.dot(kr_ref[...], place_ref[...], preferred_element_type=f32)
    v = jnp.dot(ckvn, wv_ref[...], preferred_element_type=f32)
    for h in range(MLA_HEADS):
        sl = slice(h * HEAD_W, (h + 1) * HEAD_W)
        mq_o[:, sl] = (_rope_block(q[:, sl], cm, sm, m_lo, MLA_ROPE // 2) * m_scale).astype(mq_o.dtype)
        mk_o[:, sl] = _rope_block(k[:, sl], cm, sm, m_lo, MLA_ROPE // 2).astype(mk_o.dtype)
    mv_o[...] = v.astype(mv_o.dtype)


def _prep(proj, cd, sd, cm, sm, qn, wuq, kvn, wk, wv, place, *, tm):
    t = proj.shape[0]
    tm = min(tm, t)
    row = lambda w, off: pl.BlockSpec((tm, w), lambda i, _o=off // w: (i, _o))
    full = lambda a: pl.BlockSpec(a.shape, lambda i: (0,) * a.ndim)
    tab = pl.BlockSpec((tm, LANES), lambda i: (i, 0))
    bf = jnp.bfloat16
    return pl.pallas_call(
        _prep_kernel,
        out_shape=[jax.ShapeDtypeStruct((t, 2 * DIFF_HEADS * HEAD_W), bf),
                   jax.ShapeDtypeStruct((t, DIFF_HEADS * HEAD_W), bf),
                   jax.ShapeDtypeStruct((t, MLA_HEADS * HEAD_W), bf),
                   jax.ShapeDtypeStruct((t, MLA_HEADS * HEAD_W), bf),
                   jax.ShapeDtypeStruct((t, MLA_HEADS * HEAD_W), bf)],
        grid=(t // tm,),
        in_specs=[row(512, OFF_DQ), row(512, OFF_DK), row(MLA_Q_RANK, OFF_CQ),
                  row(MLA_KV_RANK, OFF_CKV), row(LANES, OFF_KR), tab, tab, tab, tab,
                  full(qn), full(wuq), full(kvn), full(wk), full(wv), full(place)],
        out_specs=[pl.BlockSpec((tm, 2 * DIFF_HEADS * HEAD_W), lambda i: (i, 0)),
                   pl.BlockSpec((tm, DIFF_HEADS * HEAD_W), lambda i: (i, 0)),
                   pl.BlockSpec((tm, MLA_HEADS * HEAD_W), lambda i: (i, 0)),
                   pl.BlockSpec((tm, MLA_HEADS * HEAD_W), lambda i: (i, 0)),
                   pl.BlockSpec((tm, MLA_HEADS * HEAD_W), lambda i: (i, 0))],
        compiler_params=_cparams("parallel"),
        name="attn_prep",
    )(proj, proj, proj, proj, proj, cd, sd, cm, sm, qn, wuq, kvn, wk, wv, place)


def _flash_kernel(*refs, n_maps, tk, lam_init):
    if n_maps == 2:
        q_ref, k_ref, v_ref, lamv_ref, subln_ref, o_ref, m_ref, l_ref, acc_ref = refs
    else:
        q_ref, k_ref, v_ref, o_ref, m_ref, l_ref, acc_ref = refs
    f32 = jnp.float32
    tq = q_ref.shape[0]
    nk = k_ref.shape[0] // tk
    m_ref[...] = jnp.full(m_ref.shape, -jnp.inf, f32)
    l_ref[...] = jnp.zeros(l_ref.shape, f32)
    acc_ref[...] = jnp.zeros(acc_ref.shape, f32)

    def body(c, carry):
        start = pl.multiple_of(c * tk, tk)
        kc = k_ref[pl.ds(start, tk), :]
        vc = v_ref[pl.ds(start, tk), :]
        for mi in range(n_maps):
            q = q_ref[:, mi * HEAD_W:(mi + 1) * HEAD_W]
            s = lax.dot_general(q, kc, (((1,), (1,)), ((), ())), preferred_element_type=f32)
            m_prev = m_ref[mi]
            m_new = jnp.maximum(m_prev, jnp.max(s, axis=-1, keepdims=True))
            alpha = jnp.exp(m_prev - m_new)
            p = jnp.exp(s - m_new[:, :1])
            l_ref[mi] = alpha * l_ref[mi] + jnp.sum(p, axis=-1, keepdims=True)
            acc_ref[mi] = alpha * acc_ref[mi] + jnp.dot(p.astype(vc.dtype), vc, preferred_element_type=f32)
            m_ref[mi] = m_new
        return carry

    lax.fori_loop(0, nk, body, 0)

    if n_maps == 2:
        lv = lamv_ref[...]
        lam = (jnp.exp(jnp.sum(lv[0:1] * lv[1:2], axis=-1, keepdims=True))
               - jnp.exp(jnp.sum(lv[2:3] * lv[3:4], axis=-1, keepdims=True)) + lam_init)
        o = acc_ref[0] / l_ref[0] - lam * (acc_ref[1] / l_ref[1])
        o = _rms(o, subln_ref[...]) * (1.0 - lam_init)
    else:
        o = acc_ref[0] / l_ref[0]
    o_ref[...] = o.astype(o_ref.dtype)


def _flash(q, k, v, v_col0, batch, seq, heads, *, n_maps, tq, tk, lam_init=0.0, lamv=None, subln=None, name):
    tq = min(tq, seq)
    tk = min(tk, seq)
    nq = seq // tq
    vb = v_col0 // HEAD_W
    in_specs = [pl.BlockSpec((tq, n_maps * HEAD_W), lambda b, h, i: (b * nq + i, h)),
                pl.BlockSpec((seq, HEAD_W), lambda b, h, i: (b, h)),
                pl.BlockSpec((seq, HEAD_W), lambda b, h, i: (b, vb + h))]
    args = [q, k, v]
    if n_maps == 2:
        in_specs += [pl.BlockSpec(lamv.shape, lambda b, h, i: (0, 0)),
                     pl.BlockSpec(subln.shape, lambda b, h, i: (0, 0))]
        args += [lamv, subln]
    return pl.pallas_call(
        functools.partial(_flash_kernel, n_maps=n_maps, tk=tk, lam_init=lam_init),
        out_shape=jax.ShapeDtypeStruct((batch * seq, heads * HEAD_W), jnp.bfloat16),
        grid=(batch, heads, nq),
        in_specs=in_specs,
        out_specs=pl.BlockSpec((tq, HEAD_W), lambda b, h, i: (b * nq + i, h)),
        scratch_shapes=[pltpu.VMEM((n_maps, tq, HEAD_W), jnp.float32),
                        pltpu.VMEM((n_maps, tq, HEAD_W), jnp.float32),
                        pltpu.VMEM((n_maps, tq, HEAD_W), jnp.float32)],
        compiler_params=_cparams("parallel", "parallel", "arbitrary"),
        name=name,
    )(*args)


def _sigmoid(x):
    return 1.0 / (1.0 + jnp.exp(-x))


def _merge_kernel(g0_ref, g1_ref, g2_ref, ab_ref, ac_ref, ah_ref, pc_ref, ph_ref, nc_ref, nh_ref,
                  yb_ref, yc_ref, h_ref, cw_ref, wb_ref, wo_ref, o_ref, *, tiles_per_seq):
    f32 = jnp.float32
    tm = ab_ref.shape[0]
    i = pl.program_id(0)
    pos = i % tiles_per_seq
    z = ac_ref[...].astype(f32) * ah_ref[...].astype(f32)
    zp_edge = pc_ref[SUBLANES - 1:SUBLANES, :].astype(f32) * ph_ref[SUBLANES - 1:SUBLANES, :].astype(f32)
    zn_edge = nc_ref[0:1, :].astype(f32) * nh_ref[0:1, :].astype(f32)
    zp_edge = jnp.where(pos > 0, zp_edge, 0.0)
    zn_edge = jnp.where(pos < tiles_per_seq - 1, zn_edge, 0.0)
    row = lax.broadcasted_iota(jnp.int32, z.shape, 0)
    zp = jnp.where(row == 0, zp_edge, pltpu.roll(z, 1, axis=0))
    zn = jnp.where(row == tm - 1, zn_edge, pltpu.roll(z, tm - 1, axis=0))
    cw = cw_ref[...]
    y_a = ab_ref[...].astype(f32) * (cw[0:1] * zp + cw[1:2] * z + cw[2:3] * zn)

    bf = jnp.bfloat16
    merged = _sigmoid(g0_ref[...].astype(f32)) * jnp.dot(y_a.astype(bf), wb_ref[0], preferred_element_type=f32)
    merged += _sigmoid(g1_ref[...].astype(f32)) * jnp.dot(yb_ref[...], wb_ref[1], preferred_element_type=f32)
    merged += _sigmoid(g2_ref[...].astype(f32)) * jnp.dot(yc_ref[...], wb_ref[2], preferred_element_type=f32)
    o_ref[...] = h_ref[...] + jnp.dot(merged.astype(bf), wo_ref[...], preferred_element_type=f32)


def _merge(proj, y_b, y_c, h, conv_w, w_branch, w_out, seq, *, tm):
    t, d = h.shape
    tm = min(tm, seq)
    tiles_per_seq = seq // tm
    r8 = tm // SUBLANES
    nrow8 = t // SUBLANES
    row = lambda w, off: pl.BlockSpec((tm, w), lambda i, _o=off // w: (i, _o))
    prev = lambda off: pl.BlockSpec((SUBLANES, CONV_CH),
                                    lambda i, _o=off // CONV_CH: (jnp.maximum(i * r8 - 1, 0), _o))
    nxt = lambda off: pl.BlockSpec((SUBLANES, CONV_CH),
                                   lambda i, _o=off // CONV_CH: (jnp.minimum((i + 1) * r8, nrow8 - 1), _o))
    full = lambda a: pl.BlockSpec(a.shape, lambda i: (0,) * a.ndim)
    return pl.pallas_call(
        functools.partial(_merge_kernel, tiles_per_seq=tiles_per_seq),
        out_shape=jax.ShapeDtypeStruct((t, d), jnp.float32),
        grid=(t // tm,),
        in_specs=[row(d, OFF_GATES), row(d, OFF_GATES + d), row(d, OFF_GATES + 2 * d),
                  row(CONV_CH, OFF_AB), row(CONV_CH, OFF_AC), row(CONV_CH, OFF_AH),
                  prev(OFF_AC), prev(OFF_AH), nxt(OFF_AC), nxt(OFF_AH),
                  pl.BlockSpec((tm, y_b.shape[1]), lambda i: (i, 0)),
                  pl.BlockSpec((tm, y_c.shape[1]), lambda i: (i, 0)),
                  pl.BlockSpec((tm, d), lambda i: (i, 0)),
                  full(conv_w), full(w_branch), full(w_out)],
        out_specs=pl.BlockSpec((tm, d), lambda i: (i, 0)),
        compiler_params=_cparams("parallel"),
        name="gated_merge",
    )(proj, proj, proj, proj, proj, proj, proj, proj, proj, proj, y_b, y_c, h, conv_w, w_branch, w_out)


def _cross_kernel(h_ref, g_ref, wq_ref, kv_ref, wo_ref, o_ref):
    f32, bf = jnp.float32, jnp.bfloat16
    h = h_ref[...]
    d = h.shape[1]
    hd = d // CROSS_HEADS
    hn = _rms(h, g_ref[...]).astype(bf)
    q = (jnp.dot(hn, wq_ref[...], preferred_element_type=f32) * (hd ** -0.5)).astype(bf)
    outs = []
    for hh in range(CROSS_HEADS):
        kh = kv_ref[:, hh * hd:(hh + 1) * hd]
        vh = kv_ref[:, d + hh * hd:d + (hh + 1) * hd]
        s = lax.dot_general(q[:, hh * hd:(hh + 1) * hd], kh, (((1,), (1,)), ((), ())),
                            preferred_element_type=f32)
        p = jnp.exp(s - jnp.max(s, axis=-1, keepdims=True))
        p = p / jnp.sum(p, axis=-1, keepdims=True)
        outs.append(jnp.dot(p.astype(bf), vh, preferred_element_type=f32).astype(bf))
    o = jnp.concatenate(outs, axis=-1)
    o_ref[...] = h + jnp.dot(o, wo_ref[...], preferred_element_type=f32)


def _cross(h, g, w_cq, memkv, w_co, seq, mem_tokens, *, tm):
    t, d = h.shape
    tm = min(tm, seq)
    tiles_per_seq = seq // tm
    full = lambda a: pl.BlockSpec(a.shape, lambda i: (0,) * a.ndim)
    return pl.pallas_call(
        _cross_kernel,
        out_shape=jax.ShapeDtypeStruct((t, d), jnp.float32),
        grid=(t // tm,),
        in_specs=[pl.BlockSpec((tm, d), lambda i: (i, 0)), full(g), full(w_cq),
                  pl.BlockSpec((mem_tokens, 2 * d), lambda i: (i // tiles_per_seq, 0)), full(w_co)],
        out_specs=pl.BlockSpec((tm, d), lambda i: (i, 0)),
        compiler_params=_cparams("parallel"),
        name="cross_attn",
    )(h, g, w_cq, memkv, w_co)


def _topk_rows(sc, k, vals_ref, idx_ref, payload=None):
    n = sc.shape[0]
    row = lax.broadcasted_iota(jnp.int32, sc.shape, 0).astype(jnp.float32)
    for r in range(k):
        m = jnp.max(sc, axis=0, keepdims=True)
        first = jnp.min(jnp.where(sc == m, row, float(n)), axis=0, keepdims=True)
        hit = row == first
        vals_ref[r:r + 1, :] = m
        if payload is None:
            idx_ref[r:r + 1, :] = first
        else:
            idx_ref[r:r + 1, :] = jnp.sum(jnp.where(hit, payload, 0.0), axis=0, keepdims=True)
        sc = jnp.where(hit, -jnp.inf, sc)


def _route_kernel(h_ref, g_ref, wq_ref, keys_ref, xn_o, idx_o, gate_o,
                  s1_ref, i1_ref, s2_ref, i2_ref, cs_ref, ci_ref, ts_ref, ti_ref):
    f32, bf = jnp.float32, jnp.bfloat16
    xn = _rms(h_ref[...], g_ref[...])
    xn_o[...] = xn
    q = jnp.dot(xn.astype(bf), wq_ref[...], preferred_element_type=f32).astype(bf)
    kd = PEER_KEY_DIM
    for hh in range(PEER_HEADS):
        for p, (s_ref, i_ref) in enumerate(((s1_ref, i1_ref), (s2_ref, i2_ref))):
            c0 = (hh * 2 + p) * kd
            sc = lax.dot_general(keys_ref[hh, p], q[:, c0:c0 + kd], (((1,), (1,)), ((), ())),
                                 preferred_element_type=f32)
            _topk_rows(sc, PEER_TOPK, s_ref, i_ref)
        nk = float(PEER_N_KEYS)
        cs_ref[0:PEER_TOPK, :] = s1_ref[0:1, :] + s2_ref[...]
        ci_ref[0:PEER_TOPK, :] = i1_ref[0:1, :] * nk + i2_ref[...]
        off = PEER_TOPK
        for a in range(1, PEER_TOPK // 2):
            cs_ref[off:off + SUBLANES, :] = s1_ref[a:a + 1, :] + s2_ref[0:SUBLANES, :]
            ci_ref[off:off + SUBLANES, :] = i1_ref[a:a + 1, :] * nk + i2_ref[0:SUBLANES, :]
            off += SUBLANES
        cs_ref[off:off + SUBLANES, :] = s1_ref[PEER_TOPK // 2:PEER_TOPK, :] + s2_ref[0:1, :]
        ci_ref[off:off + SUBLANES, :] = i1_ref[PEER_TOPK // 2:PEER_TOPK, :] * nk + i2_ref[0:1, :]
        _topk_rows(cs_ref[...], PEER_TOPK, ts_ref, ti_ref, payload=ci_ref[...])
        ts = ts_ref[...]
        e = jnp.exp(ts - ts[0:1, :])
        gate_o[hh * PEER_TOPK:(hh + 1) * PEER_TOPK, :] = e / jnp.sum(e, axis=0, keepdims=True)
        idx_o[hh * PEER_TOPK:(hh + 1) * PEER_TOPK, :] = ti_ref[...].astype(jnp.int32)


def _route(h, g, w_q, keys, *, tm):
    t, d = h.shape
    tm = min(tm, t)
    nj = PEER_HEADS * PEER_TOPK
    full = lambda a: pl.BlockSpec(a.shape, lambda i: (0,) * a.ndim)
    assert PEER_TOPK == 2 * SUBLANES
    n_cand = PEER_TOPK + (PEER_TOPK // 2) * SUBLANES
    small = lambda: pltpu.VMEM((PEER_TOPK, tm), jnp.float32)
    big = lambda: pltpu.VMEM((n_cand, tm), jnp.float32)
    return pl.pallas_call(
        _route_kernel,
        out_shape=[jax.ShapeDtypeStruct((t, d), jnp.float32),
                   jax.ShapeDtypeStruct((nj, t), jnp.int32),
                   jax.ShapeDtypeStruct((nj, t), jnp.float32)],
        grid=(t // tm,),
        in_specs=[pl.BlockSpec((tm, d), lambda i: (i, 0)), full(g), full(w_q), full(keys)],
        out_specs=[pl.BlockSpec((tm, d), lambda i: (i, 0)),
                   pl.BlockSpec((nj, tm), lambda i: (0, i)),
                   pl.BlockSpec((nj, tm), lambda i: (0, i))],
        scratch_shapes=[small(), small(), small(), small(), big(), big(), small(), small()],
        compiler_params=_cparams("parallel"),
        name="peer_route",
    )(h, g, w_q, keys)


def _gelu(x):
    return 0.5 * x * (1.0 + lax.erf(x * (2.0 ** -0.5)))


def _experts_kernel(idx_ref, idxn_ref, gate_ref, xn_ref, h_ref, uv_hbm, o_ref, buf, sem):
    f32, bf = jnp.float32, jnp.bfloat16
    tt, d = xn_ref.shape
    nj = gate_ref.shape[1]
    nrows = tt * nj
    nchunk = d // LANES
    ngroups = nrows // SUBLANES
    i = pl.program_id(0)
    n = pl.num_programs(0)
    slot = lax.rem(i, 2)

    def issue(iref, sl):
        def body(g, carry):
            for k in range(SUBLANES):
                e = iref[0, 0, g * SUBLANES + k]
                pltpu.make_async_copy(uv_hbm.at[e], buf.at[sl, g, :, k, :], sem.at[sl]).start(priority=k % 2)
            return carry
        lax.fori_loop(0, ngroups, body, 0, unroll=2)

    @pl.when(i == 0)
    def _():
        issue(idx_ref, slot)

    @pl.when(i + 1 < n)
    def _():
        issue(idxn_ref, 1 - slot)

    pltpu.make_async_copy(buf.at[slot], buf.at[slot], sem.at[slot]).wait()
    xn = xn_ref[...].astype(bf)
    act = jnp.zeros((tt, nrows), f32)
    for c in range(nchunk):
        uc = buf[slot, :, c, :, :].reshape(nrows, LANES).astype(bf)
        act += lax.dot_general(xn[:, c * LANES:(c + 1) * LANES], uc, (((1,), (1,)), ((), ())),
                               preferred_element_type=f32)
    col = lax.broadcasted_iota(jnp.int32, (tt, nrows), 1)
    row = lax.broadcasted_iota(jnp.int32, (tt, nrows), 0)
    own = (col >= row * nj) & (col < (row + 1) * nj)
    gate = jnp.concatenate([gate_ref[...]] * tt, axis=1)
    w = jnp.where(own, gate * _gelu(act), 0.0).astype(bf)
    for c in range(nchunk):
        vc = buf[slot, :, nchunk + c, :, :].reshape(nrows, LANES).astype(bf)
        sl = slice(c * LANES, (c + 1) * LANES)
        o_ref[:, sl] = h_ref[:, sl] + jnp.dot(w, vc, preferred_element_type=f32)


def _experts(idx3, gate, xn, h, uv, *, tt):
    t, d = h.shape
    nj = gate.shape[1]
    n = t // tt
    return pl.pallas_call(
        _experts_kernel,
        out_shape=jax.ShapeDtypeStruct((t, d), jnp.float32),
        grid=(n,),
        in_specs=[pl.BlockSpec((1, 1, tt * nj), lambda i: (i, 0, 0), memory_space=pltpu.SMEM),
                  pl.BlockSpec((1, 1, tt * nj), lambda i: (jnp.minimum(i + 1, n - 1), 0, 0),
                               memory_space=pltpu.SMEM),
                  pl.BlockSpec((tt, nj), lambda i: (i, 0)),
                  pl.BlockSpec((tt, d), lambda i: (i, 0)),
                  pl.BlockSpec((tt, d), lambda i: (i, 0)),
                  pl.BlockSpec(memory_space=pl.ANY)],
        out_specs=pl.BlockSpec((tt, d), lambda i: (i, 0)),
        scratch_shapes=[pltpu.VMEM((2, tt * nj // SUBLANES, 2 * d // LANES, SUBLANES, LANES), jnp.float32),
                        pltpu.SemaphoreType.DMA((2,))],
        compiler_params=_cparams("arbitrary"),
        name="peer_experts",
    )(idx3, idx3, gate, xn, h, uv)


def _final_norm_kernel(x_ref, g_ref, o_ref):
    o_ref[...] = _rms(x_ref[...], g_ref[...])


def _final_norm(h, g, *, tm):
    t, d = h.shape
    tm = min(tm, t)
    return pl.pallas_call(
        _final_norm_kernel,
        out_shape=jax.ShapeDtypeStruct((t, d), jnp.float32),
        grid=(t // tm,),
        in_specs=[pl.BlockSpec((tm, d), lambda i: (i, 0)), pl.BlockSpec((1, d), lambda i: (0, 0))],
        out_specs=pl.BlockSpec((tm, d), lambda i: (i, 0)),
        compiler_params=_cparams("parallel"),
        name="final_norm",
    )(h, g.reshape(1, d))


def _rope_lane_tables(positions, rot_dim, period, lane0):
    half = rot_dim // 2
    inv = 1.0 / (ROPE_THETA ** (jnp.arange(0, rot_dim, 2, dtype=jnp.float32) / rot_dim))
    ang = positions.astype(jnp.float32).reshape(-1, 1) * inv[None, :]
    cos, sin = jnp.cos(ang), jnp.sin(ang)
    lane = np.arange(LANES) % period - lane0
    in_rot = (lane >= 0) & (lane < rot_dim)
    sel = np.where(in_rot, lane % half, 0)
    sign = np.where(lane < half, -1.0, 1.0).astype(np.float32)
    c = jnp.where(in_rot[None, :], cos[:, sel], 1.0)
    s = jnp.where(in_rot[None, :], sin[:, sel] * sign[None, :], 0.0)
    return c, s


def _layer_weights(l, w_in, mla_w_uq, mla_w_ukv):
    bf = jnp.bfloat16
    d = w_in.shape[1]
    offs = np.cumsum([0, CONV_CH, CONV_CH, CONV_CH, 512, 512, 512, MLA_Q_RANK, MLA_KV_RANK, MLA_ROPE])
    w = w_in[l]
    seg = lambda i: w[:, int(offs[i]):int(offs[i + 1])]
    gates = w[:, int(offs[9]):]
    kr = jnp.pad(seg(8), ((0, 0), (0, LANES - MLA_ROPE)))
    w_in_p = jnp.concatenate([gates, seg(0), seg(1), seg(2), seg(3), seg(4), seg(5), seg(6), seg(7), kr],
                             axis=1).astype(bf)
    qd = MLA_NOPE + MLA_ROPE
    wuq = mla_w_uq[l].reshape(MLA_Q_RANK, MLA_HEADS, qd)
    wuq = jnp.pad(wuq, ((0, 0), (0, 0), (0, HEAD_W - qd))).reshape(MLA_Q_RANK, MLA_HEADS * HEAD_W).astype(bf)
    wukv = mla_w_ukv[l].reshape(MLA_KV_RANK, MLA_HEADS, MLA_NOPE + MLA_V)
    wk = jnp.pad(wukv[:, :, :MLA_NOPE], ((0, 0), (0, 0), (0, HEAD_W - MLA_NOPE)))
    wk = wk.reshape(MLA_KV_RANK, MLA_HEADS * HEAD_W).astype(bf)
    wv = wukv[:, :, MLA_NOPE:].reshape(MLA_KV_RANK, MLA_HEADS * MLA_V).astype(bf)
    return w_in_p, wuq, wk, wv


def _placement():
    e = np.zeros((LANES, MLA_HEADS * HEAD_W), np.float32)
    for h in range(MLA_HEADS):
        for c in range(MLA_ROPE):
            e[c, h * HEAD_W + MLA_NOPE + c] = 1.0
    return jnp.asarray(e, jnp.bfloat16)


def kernel(x, mem, positions, norm_mix, w_in, conv_w, diff_lambda, diff_subln, mla_q_norm, mla_w_uq,
           mla_kv_norm, mla_w_ukv, w_branch, w_out, norm_cross, norm_mem, w_cq, w_ckv, w_co, norm_ffn,
           peer_w_q, peer_keys, peer_u, peer_v, final_norm):
    bf = jnp.bfloat16
    batch, seq, d = x.shape
    mem_tokens = mem.shape[1]
    depth = w_in.shape[0]
    t = batch * seq
    h = x.reshape(t, d)
    mem2 = mem.reshape(batch * mem_tokens, d)

    cd, sd = _rope_lane_tables(positions, DIFF_ROT, DIFF_HEAD_DIM, 0)
    cm, sm = _rope_lane_tables(positions, MLA_ROPE, LANES, MLA_NOPE)
    place = _placement()
    nj = PEER_HEADS * PEER_TOPK
    tt = 8

    for l in range(depth):
        lam_init = 0.8 - 0.6 * math.exp(-0.3 * l)
        w_in_p, wuq, wk, wv = _layer_weights(l, w_in, mla_w_uq, mla_w_ukv)

        proj = _norm_matmul(h, norm_mix[l], w_in_p, tm=512, tn=PROJ_W // 2, name="in_proj")
        dq, dk, mq, mk, mv = _prep(proj, cd, sd, cm, sm, mla_q_norm[l].reshape(1, -1), wuq,
                                   mla_kv_norm[l].reshape(1, -1), wk, wv, place, tm=512)
        y_b = _flash(dq, dk, proj, OFF_DV, batch, seq, DIFF_HEADS, n_maps=2, tq=ATTN_TQ, tk=ATTN_TK,
                     lam_init=lam_init, lamv=diff_lambda[l], subln=diff_subln[l].reshape(1, -1),
                     name="diff_attn")
        y_c = _flash(mq, mk, mv, 0, batch, seq, MLA_HEADS, n_maps=1, tq=ATTN_TQ, tk=ATTN_TK, name="mla_attn")
        h = _merge(proj, y_b, y_c, h, conv_w[l], w_branch[l].astype(bf), w_out[l].astype(bf), seq, tm=256)

        memkv = _norm_matmul(mem2, norm_mem[l], w_ckv[l].astype(bf), tm=256, tn=512, name="mem_kv")
        h = _cross(h, norm_cross[l].reshape(1, d), w_cq[l].astype(bf), memkv, w_co[l].astype(bf),
                   seq, mem_tokens, tm=256)

        xn, idx, gate = _route(h, norm_ffn[l].reshape(1, d), peer_w_q[l].astype(bf),
                               peer_keys[l].astype(bf), tm=256)
        idx3 = idx.T.reshape(t // tt, 1, tt * nj)
        uv = jnp.concatenate([peer_u[l].reshape(-1, d // LANES, LANES),
                              peer_v[l].reshape(-1, d // LANES, LANES)], axis=1)
        h = _experts(idx3, gate.T, xn, h, uv, tt=tt)

    out = _final_norm(h, final_norm, tm=512)
    return out.reshape(batch, seq, d)
```

```python
import functools
import math

import numpy as np
import jax
import jax.numpy as jnp
from jax import lax
from jax.experimental import pallas as pl
from jax.experimental.pallas import tpu as pltpu

EPS = 1e-6
ROPE_THETA = 500000.0
CONV_CH = 512
DIFF_HEADS = 4
DIFF_HEAD_DIM = 64
DIFF_ROT = DIFF_HEAD_DIM // 4
MLA_HEADS = 4
MLA_Q_RANK = 256
MLA_KV_RANK = 128
MLA_NOPE = 64
MLA_ROPE = 32
MLA_V = 128
N_BRANCH = 3
CROSS_HEADS = 4
PEER_HEADS = 8
PEER_N_KEYS = 128
PEER_KEY_DIM = 128
PEER_TOPK = 16

LANES = 128
SUBLANES = 8
VMEM_LIMIT_BYTES = 56 * 1024 * 1024

ATTN_TQ = 512
ATTN_TK = 8192

HEAD_W = 128
OFF_GATES = 0
OFF_AB = 3072
OFF_AC = OFF_AB + CONV_CH
OFF_AH = OFF_AC + CONV_CH
OFF_DQ = OFF_AH + CONV_CH
OFF_DK = OFF_DQ + 512
OFF_DV = OFF_DK + 512
OFF_CQ = OFF_DV + 512
OFF_CKV = OFF_CQ + MLA_Q_RANK
OFF_KR = OFF_CKV + MLA_KV_RANK
PROJ_W = OFF_KR + LANES


def _cparams(*sem):
    return pltpu.CompilerParams(dimension_semantics=sem, vmem_limit_bytes=VMEM_LIMIT_BYTES)


def _norm_matmul_kernel(x_ref, g_ref, w_ref, o_ref, xn_ref):
    @pl.when(pl.program_id(1) == 0)
    def _():
        x = x_ref[...]
        ms = jnp.mean(x * x, axis=-1, keepdims=True)
        xn_ref[...] = (x * lax.rsqrt(ms + EPS) * g_ref[...]).astype(xn_ref.dtype)

    o_ref[...] = jnp.dot(xn_ref[...], w_ref[...],
                         preferred_element_type=jnp.float32).astype(o_ref.dtype)


def _norm_matmul(x, g, w, *, tm, tn, name):
    t, k = x.shape
    n = w.shape[1]
    tm = min(tm, t)
    return pl.pallas_call(
        _norm_matmul_kernel,
        out_shape=jax.ShapeDtypeStruct((t, n), jnp.bfloat16),
        grid=(t // tm, n // tn),
        in_specs=[pl.BlockSpec((tm, k), lambda i, j: (i, 0)),
                  pl.BlockSpec((1, k), lambda i, j: (0, 0)),
                  pl.BlockSpec((k, tn), lambda i, j: (0, j))],
        out_specs=pl.BlockSpec((tm, tn), lambda i, j: (i, j)),
        scratch_shapes=[pltpu.VMEM((tm, k), jnp.bfloat16)],
        compiler_params=_cparams("parallel", "arbitrary"),
        name=name,
    )(x, g.reshape(1, k), w)


def _rope_block(xb, c, s, lo_mask, shift):
    partner = jnp.where(lo_mask, pltpu.roll(xb, LANES - shift, axis=1), pltpu.roll(xb, shift, axis=1))
    return xb * c + partner * s


def _rms(x, g):
    ms = jnp.mean(x * x, axis=-1, keepdims=True)
    return x * lax.rsqrt(ms + EPS) * g


def _prep_kernel(dq_ref, dk_ref, cq_ref, ckv_ref, kr_ref, cd_ref, sd_ref, cm_ref, sm_ref,
                 qn_ref, wuq_ref, kvn_ref, wk_ref, wv_ref, place_ref,
                 dq_o, dk_o, mq_o, mk_o, mv_o):
    f32 = jnp.float32
    tm = dq_ref.shape[0]
    lane = lax.broadcasted_iota(jnp.int32, (tm, LANES), 1)
    cd, sd, cm, sm = cd_ref[...], sd_ref[...], cm_ref[...], sm_ref[...]
    d_lo = (lane % DIFF_HEAD_DIM) < (DIFF_ROT // 2)
    m_lo = lane < (MLA_NOPE + MLA_ROPE // 2)
    d_scale = DIFF_HEAD_DIM ** -0.5
    m_scale = (MLA_NOPE + MLA_ROPE) ** -0.5
    first_half = lane < DIFF_HEAD_DIM

    for h in range(DIFF_HEADS):
        sl = slice(h * HEAD_W, (h + 1) * HEAD_W)
        qb = _rope_block(dq_ref[:, sl].astype(f32), cd, sd, d_lo, DIFF_ROT // 2) * d_scale
        kb = _rope_block(dk_ref[:, sl].astype(f32), cd, sd, d_lo, DIFF_ROT // 2)
        dq_o[:, 2 * h * HEAD_W:(2 * h + 1) * HEAD_W] = jnp.where(first_half, qb, 0.0).astype(dq_o.dtype)
        dq_o[:, (2 * h + 1) * HEAD_W:(2 * h + 2) * HEAD_W] = jnp.where(first_half, 0.0, qb).astype(dq_o.dtype)
        dk_o[:, sl] = kb.astype(dk_o.dtype)

    cqn = _rms(cq_ref[...].astype(f32), qn_ref[...]).astype(jnp.bfloat16)
    q = jnp.dot(cqn, wuq_ref[...], preferred_element_type=f32)
    ckvn = _rms(ckv_ref[...].astype(f32), kvn_ref[...]).astype(jnp.bfloat16)
    k = jnp.dot(ckvn, wk_ref[...], preferred_element_type=f32)
    k = k + jnp.dot(kr_ref[...], place_ref[...], preferred_element_type=f32)
    v = jnp.dot(ckvn, wv_ref[...], preferred_element_type=f32)
    for h in range(MLA_HEADS):
        sl = slice(h * HEAD_W, (h + 1) * HEAD_W)
        mq_o[:, sl] = (_rope_block(q[:, sl], cm, sm, m_lo, MLA_ROPE // 2) * m_scale).astype(mq_o.dtype)
        mk_o[:, sl] = _rope_block(k[:, sl], cm, sm, m_lo, MLA_ROPE // 2).astype(mk_o.dtype)
    mv_o[...] = v.astype(mv_o.dtype)


def _prep(proj, cd, sd, cm, sm, qn, wuq, kvn, wk, wv, place, *, tm):
    t = proj.shape[0]
    tm = min(tm, t)
    row = lambda w, off: pl.BlockSpec((tm, w), lambda i, _o=off // w: (i, _o))
    full = lambda a: pl.BlockSpec(a.shape, lambda i: (0,) * a.ndim)
    tab = pl.BlockSpec((tm, LANES), lambda i: (i, 0))
    bf = jnp.bfloat16
    return pl.pallas_call(
        _prep_kernel,
        out_shape=[jax.ShapeDtypeStruct((t, 2 * DIFF_HEADS * HEAD_W), bf),
                   jax.ShapeDtypeStruct((t, DIFF_HEADS * HEAD_W), bf),
                   jax.ShapeDtypeStruct((t, MLA_HEADS * HEAD_W), bf),
                   jax.ShapeDtypeStruct((t, MLA_HEADS * HEAD_W), bf),
                   jax.ShapeDtypeStruct((t, MLA_HEADS * HEAD_W), bf)],
        grid=(t // tm,),
        in_specs=[row(512, OFF_DQ), row(512, OFF_DK), row(MLA_Q_RANK, OFF_CQ),
                  row(MLA_KV_RANK, OFF_CKV), row(LANES, OFF_KR), tab, tab, tab, tab,
                  full(qn), full(wuq), full(kvn), full(wk), full(wv), full(place)],
        out_specs=[pl.BlockSpec((tm, 2 * DIFF_HEADS * HEAD_W), lambda i: (i, 0)),
                   pl.BlockSpec((tm, DIFF_HEADS * HEAD_W), lambda i: (i, 0)),
                   pl.BlockSpec((tm, MLA_HEADS * HEAD_W), lambda i: (i, 0)),
                   pl.BlockSpec((tm, MLA_HEADS * HEAD_W), lambda i: (i, 0)),
                   pl.BlockSpec((tm, MLA_HEADS * HEAD_W), lambda i: (i, 0))],
        compiler_params=_cparams("parallel"),
        name="attn_prep",
    )(proj, proj, proj, proj, proj, cd, sd, cm, sm, qn, wuq, kvn, wk, wv, place)


def _flash_kernel(*refs, n_maps, tk, lam_init):
    if n_maps == 2:
        q_ref, k_ref, v_ref, lamv_ref, subln_ref, o_ref, m_ref, l_ref, acc_ref = refs
    else:
        q_ref, k_ref, v_ref, o_ref, m_ref, l_ref, acc_ref = refs
    f32 = jnp.float32
    tq = q_ref.shape[0]
    nk = k_ref.shape[0] // tk
    m_ref[...] = jnp.full(m_ref.shape, -jnp.inf, f32)
    l_ref[...] = jnp.zeros(l_ref.shape, f32)
    acc_ref[...] = jnp.zeros(acc_ref.shape, f32)

    def body(c, carry):
        start = pl.multiple_of(c * tk, tk)
        kc = k_ref[pl.ds(start, tk), :]
        vc = v_ref[pl.ds(start, tk), :]
        for mi in range(n_maps):
            q = q_ref[:, mi * HEAD_W:(mi + 1) * HEAD_W]
            s = lax.dot_general(q, kc, (((1,), (1,)), ((), ())), preferred_element_type=f32)
            m_prev = m_ref[mi]
            m_new = jnp.maximum(m_prev, jnp.max(s, axis=-1, keepdims=True))
            alpha = jnp.exp(m_prev - m_new)
            p = jnp.exp(s - m_new[:, :1])
            l_ref[mi] = alpha * l_ref[mi] + jnp.sum(p, axis=-1, keepdims=True)
            acc_ref[mi] = alpha * acc_ref[mi] + jnp.dot(p.astype(vc.dtype), vc, preferred_element_type=f32)
            m_ref[mi] = m_new
        return carry

    lax.fori_loop(0, nk, body, 0)

    if n_maps == 2:
        lv = lamv_ref[...]
        lam = (jnp.exp(jnp.sum(lv[0:1] * lv[1:2], axis=-1, keepdims=True))
               - jnp.exp(jnp.sum(lv[2:3] * lv[3:4], axis=-1, keepdims=True)) + lam_init)
        o = acc_ref[0] / l_ref[0] - lam * (acc_ref[1] / l_ref[1])
        o = _rms(o, subln_ref[...]) * (1.0 - lam_init)
    else:
        o = acc_ref[0] / l_ref[0]
    o_ref[...] = o.astype(o_ref.dtype)


def _flash(q, k, v, v_col0, batch, seq, heads, *, n_maps, tq, tk, lam_init=0.0, lamv=None, subln=None, name):
    tq = min(tq, seq)
    tk = min(tk, seq)
    nq = seq // tq
    vb = v_col0 // HEAD_W
    in_specs = [pl.BlockSpec((tq, n_maps * HEAD_W), lambda b, h, i: (b * nq + i, h)),
                pl.BlockSpec((seq, HEAD_W), lambda b, h, i: (b, h)),
                pl.BlockSpec((seq, HEAD_W), lambda b, h, i: (b, vb + h))]
    args = [q, k, v]
    if n_maps == 2:
        in_specs += [pl.BlockSpec(lamv.shape, lambda b, h, i: (0, 0)),
                     pl.BlockSpec(subln.shape, lambda b, h, i: (0, 0))]
        args += [lamv, subln]
    return pl.pallas_call(
        functools.partial(_flash_kernel, n_maps=n_maps, tk=tk, lam_init=lam_init),
        out_shape=jax.ShapeDtypeStruct((batch * seq, heads * HEAD_W), jnp.bfloat16),
        grid=(batch, heads, nq),
        in_specs=in_specs,
        out_specs=pl.BlockSpec((tq, HEAD_W), lambda b, h, i: (b * nq + i, h)),
        scratch_shapes=[pltpu.VMEM((n_maps, tq, HEAD_W), jnp.float32),
                        pltpu.VMEM((n_maps, tq, HEAD_W), jnp.float32),
                        pltpu.VMEM((n_maps, tq, HEAD_W), jnp.float32)],
        compiler_params=_cparams("parallel", "parallel", "arbitrary"),
        name=name,
    )(*args)


def _sigmoid(x):
    return 1.0 / (1.0 + jnp.exp(-x))


def _merge_kernel(g0_ref, g1_ref, g2_ref, ab_ref, ac_ref, ah_ref, pc_ref, ph_ref, nc_ref, nh_ref,
                  yb_ref, yc_ref, h_ref, cw_ref, wb_ref, wo_ref, o_ref, *, tiles_per_seq):
    f32 = jnp.float32
    tm = ab_ref.shape[0]
    i = pl.program_id(0)
    pos = i % tiles_per_seq
    z = ac_ref[...].astype(f32) * ah_ref[...].astype(f32)
    zp_edge = pc_ref[SUBLANES - 1:SUBLANES, :].astype(f32) * ph_ref[SUBLANES - 1:SUBLANES, :].astype(f32)
    zn_edge = nc_ref[0:1, :].astype(f32) * nh_ref[0:1, :].astype(f32)
    zp_edge = jnp.where(pos > 0, zp_edge, 0.0)
    zn_edge = jnp.where(pos < tiles_per_seq - 1, zn_edge, 0.0)
    row = lax.broadcasted_iota(jnp.int32, z.shape, 0)
    zp = jnp.where(row == 0, zp_edge, pltpu.roll(z, 1, axis=0))
    zn = jnp.where(row == tm - 1, zn_edge, pltpu.roll(z, tm - 1, axis=0))
    cw = cw_ref[...]
    y_a = ab_ref[...].astype(f32) * (cw[0:1] * zp + cw[1:2] * z + cw[2:3] * zn)

    bf = jnp.bfloat16
    merged = _sigmoid(g0_ref[...].astype(f32)) * jnp.dot(y_a.astype(bf), wb_ref[0], preferred_element_type=f32)
    merged += _sigmoid(g1_ref[...].astype(f32)) * jnp.dot(yb_ref[...], wb_ref[1], preferred_element_type=f32)
    merged += _sigmoid(g2_ref[...].astype(f32)) * jnp.dot(yc_ref[...], wb_ref[2], preferred_element_type=f32)
    o_ref[...] = h_ref[...] + jnp.dot(merged.astype(bf), wo_ref[...], preferred_element_type=f32)


def _merge(proj, y_b, y_c, h, conv_w, w_branch, w_out, seq, *, tm):
    t, d = h.shape
    tm = min(tm, seq)
    tiles_per_seq = seq // tm
    r8 = tm // SUBLANES
    nrow8 = t // SUBLANES
    row = lambda w, off: pl.BlockSpec((tm, w), lambda i, _o=off // w: (i, _o))
    prev = lambda off: pl.BlockSpec((SUBLANES, CONV_CH),
                                    lambda i, _o=off // CONV_CH: (jnp.maximum(i * r8 - 1, 0), _o))
    nxt = lambda off: pl.BlockSpec((SUBLANES, CONV_CH),
                                   lambda i, _o=off // CONV_CH: (jnp.minimum((i + 1) * r8, nrow8 - 1), _o))
    full = lambda a: pl.BlockSpec(a.shape, lambda i: (0,) * a.ndim)
    return pl.pallas_call(
        functools.partial(_merge_kernel, tiles_per_seq=tiles_per_seq),
        out_shape=jax.ShapeDtypeStruct((t, d), jnp.float32),
        grid=(t // tm,),
        in_specs=[row(d, OFF_GATES), row(d, OFF_GATES + d), row(d, OFF_GATES + 2 * d),
                  row(CONV_CH, OFF_AB), row(CONV_CH, OFF_AC), row(CONV_CH, OFF_AH),
                  prev(OFF_AC), prev(OFF_AH), nxt(OFF_AC), nxt(OFF_AH),
                  pl.BlockSpec((tm, y_b.shape[1]), lambda i: (i, 0)),
                  pl.BlockSpec((tm, y_c.shape[1]), lambda i: (i, 0)),
                  pl.BlockSpec((tm, d), lambda i: (i, 0)),
                  full(conv_w), full(w_branch), full(w_out)],
        out_specs=pl.BlockSpec((tm, d), lambda i: (i, 0)),
        compiler_params=_cparams("parallel"),
        name="gated_merge",
    )(proj, proj, proj, proj, proj, proj, proj, proj, proj, proj, y_b, y_c, h, conv_w, w_branch, w_out)


def _cross_kernel(h_ref, g_ref, wq_ref, kv_ref, wo_ref, o_ref):
    f32, bf = jnp.float32, jnp.bfloat16
    h = h_ref[...]
    d = h.shape[1]
    hd = d // CROSS_HEADS
    hn = _rms(h, g_ref[...]).astype(bf)
    q = (jnp.dot(hn, wq_ref[...], preferred_element_type=f32) * (hd ** -0.5)).astype(bf)
    outs = []
    for hh in range(CROSS_HEADS):
        kh = kv_ref[:, hh * hd:(hh + 1) * hd]
        vh = kv_ref[:, d + hh * hd:d + (hh + 1) * hd]
        s = lax.dot_general(q[:, hh * hd:(hh + 1) * hd], kh, (((1,), (1,)), ((), ())),
                            preferred_element_type=f32)
        p = jnp.exp(s - jnp.max(s, axis=-1, keepdims=True))
        p = p / jnp.sum(p, axis=-1, keepdims=True)
        outs.append(jnp.dot(p.astype(bf), vh, preferred_element_type=f32).astype(bf))
    o = jnp.concatenate(outs, axis=-1)
    o_ref[...] = h + jnp.dot(o, wo_ref[...], preferred_element_type=f32)


def _cross(h, g, w_cq, memkv, w_co, seq, mem_tokens, *, tm):
    t, d = h.shape
    tm = min(tm, seq)
    tiles_per_seq = seq // tm
    full = lambda a: pl.BlockSpec(a.shape, lambda i: (0,) * a.ndim)
    return pl.pallas_call(
        _cross_kernel,
        out_shape=jax.ShapeDtypeStruct((t, d), jnp.float32),
        grid=(t // tm,),
        in_specs=[pl.BlockSpec((tm, d), lambda i: (i, 0)), full(g), full(w_cq),
                  pl.BlockSpec((mem_tokens, 2 * d), lambda i: (i // tiles_per_seq, 0)), full(w_co)],
        out_specs=pl.BlockSpec((tm, d), lambda i: (i, 0)),
        compiler_params=_cparams("parallel"),
        name="cross_attn",
    )(h, g, w_cq, memkv, w_co)


def _topk_rows(sc, k, vals_ref, idx_ref, payload=None):
    n = sc.shape[0]
    row = lax.broadcasted_iota(jnp.int32, sc.shape, 0).astype(jnp.float32)
    for r in range(k):
        m = jnp.max(sc, axis=0, keepdims=True)
        first = jnp.min(jnp.where(sc == m, row, float(n)), axis=0, keepdims=True)
        hit = row == first
        vals_ref[r:r + 1, :] = m
        if payload is None:
            idx_ref[r:r + 1, :] = first
        else:
            idx_ref[r:r + 1, :] = jnp.sum(jnp.where(hit, payload, 0.0), axis=0, keepdims=True)
        sc = jnp.where(hit, -jnp.inf, sc)


def _route_kernel(h_ref, g_ref, wq_ref, keys_ref, xn_o, idx_o, gate_o,
                  s1_ref, i1_ref, s2_ref, i2_ref, cs_ref, ci_ref, ts_ref, ti_ref):
    f32, bf = jnp.float32, jnp.bfloat16
    xn = _rms(h_ref[...], g_ref[...])
    xn_o[...] = xn
    q = jnp.dot(xn.astype(bf), wq_ref[...], preferred_element_type=f32).astype(bf)
    kd = PEER_KEY_DIM
    for hh in range(PEER_HEADS):
        for p, (s_ref, i_ref) in enumerate(((s1_ref, i1_ref), (s2_ref, i2_ref))):
            c0 = (hh * 2 + p) * kd
            sc = lax.dot_general(keys_ref[hh, p], q[:, c0:c0 + kd], (((1,), (1,)), ((), ())),
                                 preferred_element_type=f32)
            _topk_rows(sc, PEER_TOPK, s_ref, i_ref)
        nk = float(PEER_N_KEYS)
        cs_ref[0:PEER_TOPK, :] = s1_ref[0:1, :] + s2_ref[...]
        ci_ref[0:PEER_TOPK, :] = i1_ref[0:1, :] * nk + i2_ref[...]
        off = PEER_TOPK
        for a in range(1, PEER_TOPK // 2):
            cs_ref[off:off + SUBLANES, :] = s1_ref[a:a + 1, :] + s2_ref[0:SUBLANES, :]
            ci_ref[off:off + SUBLANES, :] = i1_ref[a:a + 1, :] * nk + i2_ref[0:SUBLANES, :]
            off += SUBLANES
        cs_ref[off:off + SUBLANES, :] = s1_ref[PEER_TOPK // 2:PEER_TOPK, :] + s2_ref[0:1, :]
        ci_ref[off:off + SUBLANES, :] = i1_ref[PEER_TOPK // 2:PEER_TOPK, :] * nk + i2_ref[0:1, :]
        _topk_rows(cs_ref[...], PEER_TOPK, ts_ref, ti_ref, payload=ci_ref[...])
        ts = ts_ref[...]
        e = jnp.exp(ts - ts[0:1, :])
        gate_o[hh * PEER_TOPK:(hh + 1) * PEER_TOPK, :] = e / jnp.sum(e, axis=0, keepdims=True)
        idx_o[hh * PEER_TOPK:(hh + 1) * PEER_TOPK, :] = ti_ref[...].astype(jnp.int32)


def _route(h, g, w_q, keys, *, tm):
    t, d = h.shape
    tm = min(tm, t)
    nj = PEER_HEADS * PEER_TOPK
    full = lambda a: pl.BlockSpec(a.shape, lambda i: (0,) * a.ndim)
    assert PEER_TOPK == 2 * SUBLANES
    n_cand = PEER_TOPK + (PEER_TOPK // 2) * SUBLANES
    small = lambda: pltpu.VMEM((PEER_TOPK, tm), jnp.float32)
    big = lambda: pltpu.VMEM((n_cand, tm), jnp.float32)
    return pl.pallas_call(
        _route_kernel,
        out_shape=[jax.ShapeDtypeStruct((t, d), jnp.float32),
                   jax.ShapeDtypeStruct((nj, t), jnp.int32),
                   jax.ShapeDtypeStruct((nj, t), jnp.float32)],
        grid=(t // tm,),
        in_specs=[pl.BlockSpec((tm, d), lambda i: (i, 0)), full(g), full(w_q), full(keys)],
        out_specs=[pl.BlockSpec((tm, d), lambda i: (i, 0)),
                   pl.BlockSpec((nj, tm), lambda i: (0, i)),
                   pl.BlockSpec((nj, tm), lambda i: (0, i))],
        scratch_shapes=[small(), small(), small(), small(), big(), big(), small(), small()],
        compiler_params=_cparams("parallel"),
        name="peer_route",
    )(h, g, w_q, keys)


def _gelu(x):
    return 0.5 * x * (1.0 + lax.erf(x * (2.0 ** -0.5)))


def _experts_kernel(idx_ref, idxn_ref, gate_ref, xn_ref, h_ref, uv_hbm, o_ref, buf, sem):
    f32, bf = jnp.float32, jnp.bfloat16
    tt, d = xn_ref.shape
    nj = gate_ref.shape[1]
    nrows = tt * nj
    nchunk = d // LANES
    ngroups = nrows // SUBLANES
    i = pl.program_id(0)
    n = pl.num_programs(0)
    slot = lax.rem(i, 2)

    def start_group(iref, sl, g):
        for k in range(SUBLANES):
            e = iref[0, 0, g * SUBLANES + k]
            pltpu.make_async_copy(uv_hbm.at[e], buf.at[sl, g, :, k, :], sem.at[sl]).start(priority=k % 2)

    def wait_slot(sl):
        pltpu.make_async_copy(buf.at[sl], buf.at[sl], sem.at[sl]).wait()

    @pl.when(i == 0)
    def _():
        def body(g, carry):
            start_group(idx_ref, slot, g)
            return carry
        lax.fori_loop(0, ngroups, body, 0)

    wait_slot(slot)

    nphase = 2 * nchunk
    per_phase = ngroups // nphase

    def start_phase(ph):
        for g in range(ph * per_phase, (ph + 1) * per_phase):
            start_group(idxn_ref, 1 - slot, g)

    def words(c):
        return buf[slot, :, c, :, :].reshape(nrows, LANES)

    xn = xn_ref[...].astype(bf)
    act = jnp.zeros((tt, nrows), f32)
    for c in range(nchunk):
        uc = lax.bitcast_convert_type(words(c) & jnp.uint32(0xFFFF0000), f32).astype(bf)
        act += lax.dot_general(xn[:, c * LANES:(c + 1) * LANES], uc, (((1,), (1,)), ((), ())),
                               preferred_element_type=f32)
        start_phase(c)
    col = lax.broadcasted_iota(jnp.int32, (tt, nrows), 1)
    row = lax.broadcasted_iota(jnp.int32, (tt, nrows), 0)
    own = (col >= row * nj) & (col < (row + 1) * nj)
    gate = jnp.concatenate([gate_ref[...]] * tt, axis=1)
    w = jnp.where(own, gate * _gelu(act), 0.0).astype(bf)
    for c in range(nchunk):
        vc = lax.bitcast_convert_type(words(c) << 16, f32).astype(bf)
        sl = slice(c * LANES, (c + 1) * LANES)
        o_ref[:, sl] = h_ref[:, sl] + jnp.dot(w, vc, preferred_element_type=f32)
        start_phase(nchunk + c)

    @pl.when(i == n - 1)
    def _():
        wait_slot(1 - slot)


def _experts(idx3, gate, xn, h, uv, *, tt):
    t, d = h.shape
    nj = gate.shape[1]
    n = t // tt
    assert (tt * nj // SUBLANES) % (2 * d // LANES) == 0
    return pl.pallas_call(
        _experts_kernel,
        out_shape=jax.ShapeDtypeStruct((t, d), jnp.float32),
        grid=(n,),
        in_specs=[pl.BlockSpec((1, 1, tt * nj), lambda i: (i, 0, 0), memory_space=pltpu.SMEM),
                  pl.BlockSpec((1, 1, tt * nj), lambda i: (jnp.minimum(i + 1, n - 1), 0, 0),
                               memory_space=pltpu.SMEM),
                  pl.BlockSpec((tt, nj), lambda i: (i, 0)),
                  pl.BlockSpec((tt, d), lambda i: (i, 0)),
                  pl.BlockSpec((tt, d), lambda i: (i, 0)),
                  pl.BlockSpec(memory_space=pl.ANY)],
        out_specs=pl.BlockSpec((tt, d), lambda i: (i, 0)),
        scratch_shapes=[pltpu.VMEM((2, tt * nj // SUBLANES, d // LANES, SUBLANES, LANES), jnp.uint32),
                        pltpu.SemaphoreType.DMA((2,))],
        compiler_params=_cparams("arbitrary"),
        name="peer_experts",
    )(idx3, idx3, gate, xn, h, uv)


def _final_norm_kernel(x_ref, g_ref, o_ref):
    o_ref[...] = _rms(x_ref[...], g_ref[...])


def _final_norm(h, g, *, tm):
    t, d = h.shape
    tm = min(tm, t)
    return pl.pallas_call(
        _final_norm_kernel,
        out_shape=jax.ShapeDtypeStruct((t, d), jnp.float32),
        grid=(t // tm,),
        in_specs=[pl.BlockSpec((tm, d), lambda i: (i, 0)), pl.BlockSpec((1, d), lambda i: (0, 0))],
        out_specs=pl.BlockSpec((tm, d), lambda i: (i, 0)),
        compiler_params=_cparams("parallel"),
        name="final_norm",
    )(h, g.reshape(1, d))


def _rope_lane_tables(positions, rot_dim, period, lane0):
    half = rot_dim // 2
    inv = 1.0 / (ROPE_THETA ** (jnp.arange(0, rot_dim, 2, dtype=jnp.float32) / rot_dim))
    ang = positions.astype(jnp.float32).reshape(-1, 1) * inv[None, :]
    cos, sin = jnp.cos(ang), jnp.sin(ang)
    lane = np.arange(LANES) % period - lane0
    in_rot = (lane >= 0) & (lane < rot_dim)
    sel = np.where(in_rot, lane % half, 0)
    sign = np.where(lane < half, -1.0, 1.0).astype(np.float32)
    c = jnp.where(in_rot[None, :], cos[:, sel], 1.0)
    s = jnp.where(in_rot[None, :], sin[:, sel] * sign[None, :], 0.0)
    return c, s


def _layer_weights(l, w_in, mla_w_uq, mla_w_ukv):
    bf = jnp.bfloat16
    d = w_in.shape[1]
    offs = np.cumsum([0, CONV_CH, CONV_CH, CONV_CH, 512, 512, 512, MLA_Q_RANK, MLA_KV_RANK, MLA_ROPE])
    w = w_in[l]
    seg = lambda i: w[:, int(offs[i]):int(offs[i + 1])]
    gates = w[:, int(offs[9]):]
    kr = jnp.pad(seg(8), ((0, 0), (0, LANES - MLA_ROPE)))
    w_in_p = jnp.concatenate([gates, seg(0), seg(1), seg(2), seg(3), seg(4), seg(5), seg(6), seg(7), kr],
                             axis=1).astype(bf)
    qd = MLA_NOPE + MLA_ROPE
    wuq = mla_w_uq[l].reshape(MLA_Q_RANK, MLA_HEADS, qd)
    wuq = jnp.pad(wuq, ((0, 0), (0, 0), (0, HEAD_W - qd))).reshape(MLA_Q_RANK, MLA_HEADS * HEAD_W).astype(bf)
    wukv = mla_w_ukv[l].reshape(MLA_KV_RANK, MLA_HEADS, MLA_NOPE + MLA_V)
    wk = jnp.pad(wukv[:, :, :MLA_NOPE], ((0, 0), (0, 0), (0, HEAD_W - MLA_NOPE)))
    wk = wk.reshape(MLA_KV_RANK, MLA_HEADS * HEAD_W).astype(bf)
    wv = wukv[:, :, MLA_NOPE:].reshape(MLA_KV_RANK, MLA_HEADS * MLA_V).astype(bf)
    return w_in_p, wuq, wk, wv


def _pack_tables(u, v):
    hi = lax.bitcast_convert_type(u.astype(jnp.bfloat16), jnp.uint16).astype(jnp.uint32)
    lo = lax.bitcast_convert_type(v.astype(jnp.bfloat16), jnp.uint16).astype(jnp.uint32)
    return ((hi << 16) | lo).reshape(u.shape[0], u.shape[1] // LANES, LANES)


def _placement():
    e = np.zeros((LANES, MLA_HEADS * HEAD_W), np.float32)
    for h in range(MLA_HEADS):
        for c in range(MLA_ROPE):
            e[c, h * HEAD_W + MLA_NOPE + c] = 1.0
    return jnp.asarray(e, jnp.bfloat16)


def kernel(x, mem, positions, norm_mix, w_in, conv_w, diff_lambda, diff_subln, mla_q_norm, mla_w_uq,
           mla_kv_norm, mla_w_ukv, w_branch, w_out, norm_cross, norm_mem, w_cq, w_ckv, w_co, norm_ffn,
           peer_w_q, peer_keys, peer_u, peer_v, final_norm):
    bf = jnp.bfloat16
    batch, seq, d = x.shape
    mem_tokens = mem.shape[1]
    depth = w_in.shape[0]
    t = batch * seq
    h = x.reshape(t, d)
    mem2 = mem.reshape(batch * mem_tokens, d)

    cd, sd = _rope_lane_tables(positions, DIFF_ROT, DIFF_HEAD_DIM, 0)
    cm, sm = _rope_lane_tables(positions, MLA_ROPE, LANES, MLA_NOPE)
    place = _placement()
    nj = PEER_HEADS * PEER_TOPK
    tt = 16

    for l in range(depth):
        lam_init = 0.8 - 0.6 * math.exp(-0.3 * l)
        w_in_p, wuq, wk, wv = _layer_weights(l, w_in, mla_w_uq, mla_w_ukv)

        proj = _norm_matmul(h, norm_mix[l], w_in_p, tm=512, tn=PROJ_W // 2, name="in_proj")
        dq, dk, mq, mk, mv = _prep(proj, cd, sd, cm, sm, mla_q_norm[l].reshape(1, -1), wuq,
                                   mla_kv_norm[l].reshape(1, -1), wk, wv, place, tm=512)
        y_b = _flash(dq, dk, proj, OFF_DV, batch, seq, DIFF_HEADS, n_maps=2, tq=ATTN_TQ, tk=ATTN_TK,
                     lam_init=lam_init, lamv=diff_lambda[l], subln=diff_subln[l].reshape(1, -1),
                     name="diff_attn")
        y_c = _flash(mq, mk, mv, 0, batch, seq, MLA_HEADS, n_maps=1, tq=ATTN_TQ, tk=ATTN_TK, name="mla_attn")
        h = _merge(proj, y_b, y_c, h, conv_w[l], w_branch[l].astype(bf), w_out[l].astype(bf), seq, tm=256)

        memkv = _norm_matmul(mem2, norm_mem[l], w_ckv[l].astype(bf), tm=256, tn=512, name="mem_kv")
        h = _cross(h, norm_cross[l].reshape(1, d), w_cq[l].astype(bf), memkv, w_co[l].astype(bf),
                   seq, mem_tokens, tm=256)

        xn, idx, gate = _route(h, norm_ffn[l].reshape(1, d), peer_w_q[l].astype(bf),
                               peer_keys[l].astype(bf), tm=256)
        idx3 = idx.T.reshape(t // tt, 1, tt * nj)
        uv = _pack_tables(peer_u[l], peer_v[l])
        h = _experts(idx3, gate.T, xn, h, uv, tt=tt)

    out = _final_norm(h, final_norm, tm=512)
    return out.reshape(batch, seq, d)
```

```python
import functools
import math

import numpy as np
import jax
import jax.numpy as jnp
from jax import lax
from jax.experimental import pallas as pl
from jax.experimental.pallas import tpu as pltpu

EPS = 1e-6
ROPE_THETA = 500000.0
CONV_CH = 512
DIFF_HEADS = 4
DIFF_HEAD_DIM = 64
DIFF_ROT = DIFF_HEAD_DIM // 4
MLA_HEADS = 4
MLA_Q_RANK = 256
MLA_KV_RANK = 128
MLA_NOPE = 64
MLA_ROPE = 32
MLA_V = 128
N_BRANCH = 3
CROSS_HEADS = 4
PEER_HEADS = 8
PEER_N_KEYS = 128
PEER_KEY_DIM = 128
PEER_TOPK = 16

LANES = 128
SUBLANES = 8
VMEM_LIMIT_BYTES = 56 * 1024 * 1024

ATTN_TQ = 512
ATTN_TK = 8192

HEAD_W = 128
OFF_GATES = 0
OFF_AB = 3072
OFF_AC = OFF_AB + CONV_CH
OFF_AH = OFF_AC + CONV_CH
OFF_DQ = OFF_AH + CONV_CH
OFF_DK = OFF_DQ + 512
OFF_DV = OFF_DK + 512
OFF_CQ = OFF_DV + 512
OFF_CKV = OFF_CQ + MLA_Q_RANK
OFF_KR = OFF_CKV + MLA_KV_RANK
PROJ_W = OFF_KR + LANES


def _cparams(*sem):
    return pltpu.CompilerParams(dimension_semantics=sem, vmem_limit_bytes=VMEM_LIMIT_BYTES)


def _norm_matmul_kernel(x_ref, g_ref, w_ref, o_ref, xn_ref):
    @pl.when(pl.program_id(1) == 0)
    def _():
        x = x_ref[...]
        ms = jnp.mean(x * x, axis=-1, keepdims=True)
        xn_ref[...] = (x * lax.rsqrt(ms + EPS) * g_ref[...]).astype(xn_ref.dtype)

    o_ref[...] = jnp.dot(xn_ref[...], w_ref[...],
                         preferred_element_type=jnp.float32).astype(o_ref.dtype)


def _norm_matmul(x, g, w, *, tm, tn, name):
    t, k = x.shape
    n = w.shape[1]
    tm = min(tm, t)
    return pl.pallas_call(
        _norm_matmul_kernel,
        out_shape=jax.ShapeDtypeStruct((t, n), jnp.bfloat16),
        grid=(t // tm, n // tn),
        in_specs=[pl.BlockSpec((tm, k), lambda i, j: (i, 0)),
                  pl.BlockSpec((1, k), lambda i, j: (0, 0)),
                  pl.BlockSpec((k, tn), lambda i, j: (0, j))],
        out_specs=pl.BlockSpec((tm, tn), lambda i, j: (i, j)),
        scratch_shapes=[pltpu.VMEM((tm, k), jnp.bfloat16)],
        compiler_params=_cparams("parallel", "arbitrary"),
        name=name,
    )(x, g.reshape(1, k), w)


def _rope_block(xb, c, s, lo_mask, shift):
    partner = jnp.where(lo_mask, pltpu.roll(xb, LANES - shift, axis=1), pltpu.roll(xb, shift, axis=1))
    return xb * c + partner * s


def _rms(x, g):
    ms = jnp.mean(x * x, axis=-1, keepdims=True)
    return x * lax.rsqrt(ms + EPS) * g


def _prep_kernel(dq_ref, dk_ref, cq_ref, ckv_ref, kr_ref, cd_ref, sd_ref, cm_ref, sm_ref,
                 qn_ref, wuq_ref, kvn_ref, wk_ref, wv_ref, place_ref,
                 dq_o, dk_o, mq_o, mk_o, mv_o):
    f32 = jnp.float32
    tm = dq_ref.shape[0]
    lane = lax.broadcasted_iota(jnp.int32, (tm, LANES), 1)
    cd, sd, cm, sm = cd_ref[...], sd_ref[...], cm_ref[...], sm_ref[...]
    d_lo = (lane % DIFF_HEAD_DIM) < (DIFF_ROT // 2)
    m_lo = lane < (MLA_NOPE + MLA_ROPE // 2)
    d_scale = DIFF_HEAD_DIM ** -0.5
    m_scale = (MLA_NOPE + MLA_ROPE) ** -0.5
    first_half = lane < DIFF_HEAD_DIM

    for h in range(DIFF_HEADS):
        sl = slice(h * HEAD_W, (h + 1) * HEAD_W)
        qb = _rope_block(dq_ref[:, sl].astype(f32), cd, sd, d_lo, DIFF_ROT // 2) * d_scale
        kb = _rope_block(dk_ref[:, sl].astype(f32), cd, sd, d_lo, DIFF_ROT // 2)
        dq_o[:, 2 * h * HEAD_W:(2 * h + 1) * HEAD_W] = jnp.where(first_half, qb, 0.0).astype(dq_o.dtype)
        dq_o[:, (2 * h + 1) * HEAD_W:(2 * h + 2) * HEAD_W] = jnp.where(first_half, 0.0, qb).astype(dq_o.dtype)
        dk_o[:, sl] = kb.astype(dk_o.dtype)

    cqn = _rms(cq_ref[...].astype(f32), qn_ref[...]).astype(jnp.bfloat16)
    q = jnp.dot(cqn, wuq_ref[...], preferred_element_type=f32)
    ckvn = _rms(ckv_ref[...].astype(f32), kvn_ref[...]).astype(jnp.bfloat16)
    k = jnp.dot(ckvn, wk_ref[...], preferred_element_type=f32)
    k = k + jnp.dot(kr_ref[...], place_ref[...], preferred_element_type=f32)
    v = jnp.dot(ckvn, wv_ref[...], preferred_element_type=f32)
    for h in range(MLA_HEADS):
        sl = slice(h * HEAD_W, (h + 1) * HEAD_W)
        mq_o[:, sl] = (_rope_block(q[:, sl], cm, sm, m_lo, MLA_ROPE // 2) * m_scale).astype(mq_o.dtype)
        mk_o[:, sl] = _rope_block(k[:, sl], cm, sm, m_lo, MLA_ROPE // 2).astype(mk_o.dtype)
    mv_o[...] = v.astype(mv_o.dtype)


def _prep(proj, cd, sd, cm, sm, qn, wuq, kvn, wk, wv, place, *, tm):
    t = proj.shape[0]
    tm = min(tm, t)
    row = lambda w, off: pl.BlockSpec((tm, w), lambda i, _o=off // w: (i, _o))
    full = lambda a: pl.BlockSpec(a.shape, lambda i: (0,) * a.ndim)
    tab = pl.BlockSpec((tm, LANES), lambda i: (i, 0))
    bf = jnp.bfloat16
    return pl.pallas_call(
        _prep_kernel,
        out_shape=[jax.ShapeDtypeStruct((t, 2 * DIFF_HEADS * HEAD_W), bf),
                   jax.ShapeDtypeStruct((t, DIFF_HEADS * HEAD_W), bf),
                   jax.ShapeDtypeStruct((t, MLA_HEADS * HEAD_W), bf),
                   jax.ShapeDtypeStruct((t, MLA_HEADS * HEAD_W), bf),
                   jax.ShapeDtypeStruct((t, MLA_HEADS * HEAD_W), bf)],
        grid=(t // tm,),
        in_specs=[row(512, OFF_DQ), row(512, OFF_DK), row(MLA_Q_RANK, OFF_CQ),
                  row(MLA_KV_RANK, OFF_CKV), row(LANES, OFF_KR), tab, tab, tab, tab,
                  full(qn), full(wuq), full(kvn), full(wk), full(wv), full(place)],
        out_specs=[pl.BlockSpec((tm, 2 * DIFF_HEADS * HEAD_W), lambda i: (i, 0)),
                   pl.BlockSpec((tm, DIFF_HEADS * HEAD_W), lambda i: (i, 0)),
                   pl.BlockSpec((tm, MLA_HEADS * HEAD_W), lambda i: (i, 0)),
                   pl.BlockSpec((tm, MLA_HEADS * HEAD_W), lambda i: (i, 0)),
                   pl.BlockSpec((tm, MLA_HEADS * HEAD_W), lambda i: (i, 0))],
        compiler_params=_cparams("parallel"),
        name="attn_prep",
    )(proj, proj, proj, proj, proj, cd, sd, cm, sm, qn, wuq, kvn, wk, wv, place)


def _flash_kernel(*refs, n_maps, tk, lam_init):
    if n_maps == 2:
        q_ref, k_ref, v_ref, lamv_ref, subln_ref, o_ref, m_ref, l_ref, acc_ref = refs
    else:
        q_ref, k_ref, v_ref, o_ref, m_ref, l_ref, acc_ref = refs
    f32 = jnp.float32
    tq = q_ref.shape[0]
    nk = k_ref.shape[0] // tk
    m_ref[...] = jnp.full(m_ref.shape, -jnp.inf, f32)
    l_ref[...] = jnp.zeros(l_ref.shape, f32)
    acc_ref[...] = jnp.zeros(acc_ref.shape, f32)

    def body(c, carry):
        start = pl.multiple_of(c * tk, tk)
        kc = k_ref[pl.ds(start, tk), :]
        vc = v_ref[pl.ds(start, tk), :]
        for mi in range(n_maps):
            q = q_ref[:, mi * HEAD_W:(mi + 1) * HEAD_W]
            s = lax.dot_general(q, kc, (((1,), (1,)), ((), ())), preferred_element_type=f32)
            m_prev = m_ref[mi]
            m_new = jnp.maximum(m_prev, jnp.max(s, axis=-1, keepdims=True))
            alpha = jnp.exp(m_prev - m_new)
            p = jnp.exp(s - m_new[:, :1])
            l_ref[mi] = alpha * l_ref[mi] + jnp.sum(p, axis=-1, keepdims=True)
            acc_ref[mi] = alpha * acc_ref[mi] + jnp.dot(p.astype(vc.dtype), vc, preferred_element_type=f32)
            m_ref[mi] = m_new
        return carry

    lax.fori_loop(0, nk, body, 0)

    if n_maps == 2:
        lv = lamv_ref[...]
        lam = (jnp.exp(jnp.sum(lv[0:1] * lv[1:2], axis=-1, keepdims=True))
               - jnp.exp(jnp.sum(lv[2:3] * lv[3:4], axis=-1, keepdims=True)) + lam_init)
        o = acc_ref[0] / l_ref[0] - lam * (acc_ref[1] / l_ref[1])
        o = _rms(o, subln_ref[...]) * (1.0 - lam_init)
    else:
        o = acc_ref[0] / l_ref[0]
    o_ref[...] = o.astype(o_ref.dtype)


def _flash(q, k, v, v_col0, batch, seq, heads, *, n_maps, tq, tk, lam_init=0.0, lamv=None, subln=None, name):
    tq = min(tq, seq)
    tk = min(tk, seq)
    nq = seq // tq
    vb = v_col0 // HEAD_W
    in_specs = [pl.BlockSpec((tq, n_maps * HEAD_W), lambda b, h, i: (b * nq + i, h)),
                pl.BlockSpec((seq, HEAD_W), lambda b, h, i: (b, h)),
                pl.BlockSpec((seq, HEAD_W), lambda b, h, i: (b, vb + h))]
    args = [q, k, v]
    if n_maps == 2:
        in_specs += [pl.BlockSpec(lamv.shape, lambda b, h, i: (0, 0)),
                     pl.BlockSpec(subln.shape, lambda b, h, i: (0, 0))]
        args += [lamv, subln]
    return pl.pallas_call(
        functools.partial(_flash_kernel, n_maps=n_maps, tk=tk, lam_init=lam_init),
        out_shape=jax.ShapeDtypeStruct((batch * seq, heads * HEAD_W), jnp.bfloat16),
        grid=(batch, heads, nq),
        in_specs=in_specs,
        out_specs=pl.BlockSpec((tq, HEAD_W), lambda b, h, i: (b * nq + i, h)),
        scratch_shapes=[pltpu.VMEM((n_maps, tq, HEAD_W), jnp.float32),
                        pltpu.VMEM((n_maps, tq, HEAD_W), jnp.float32),
                        pltpu.VMEM((n_maps, tq, HEAD_W), jnp.float32)],
        compiler_params=_cparams("parallel", "parallel", "arbitrary"),
        name=name,
    )(*args)


def _sigmoid(x):
    return 1.0 / (1.0 + jnp.exp(-x))


def _merge_kernel(g0_ref, g1_ref, g2_ref, ab_ref, ac_ref, ah_ref, pc_ref, ph_ref, nc_ref, nh_ref,
                  yb_ref, yc_ref, h_ref, cw_ref, wb_ref, wo_ref, o_ref, *, tiles_per_seq):
    f32 = jnp.float32
    tm = ab_ref.shape[0]
    i = pl.program_id(0)
    pos = i % tiles_per_seq
    z = ac_ref[...].astype(f32) * ah_ref[...].astype(f32)
    zp_edge = pc_ref[SUBLANES - 1:SUBLANES, :].astype(f32) * ph_ref[SUBLANES - 1:SUBLANES, :].astype(f32)
    zn_edge = nc_ref[0:1, :].astype(f32) * nh_ref[0:1, :].astype(f32)
    zp_edge = jnp.where(pos > 0, zp_edge, 0.0)
    zn_edge = jnp.where(pos < tiles_per_seq - 1, zn_edge, 0.0)
    row = lax.broadcasted_iota(jnp.int32, z.shape, 0)
    zp = jnp.where(row == 0, zp_edge, pltpu.roll(z, 1, axis=0))
    zn = jnp.where(row == tm - 1, zn_edge, pltpu.roll(z, tm - 1, axis=0))
    cw = cw_ref[...]
    y_a = ab_ref[...].astype(f32) * (cw[0:1] * zp + cw[1:2] * z + cw[2:3] * zn)

    bf = jnp.bfloat16
    merged = _sigmoid(g0_ref[...].astype(f32)) * jnp.dot(y_a.astype(bf), wb_ref[0], preferred_element_type=f32)
    merged += _sigmoid(g1_ref[...].astype(f32)) * jnp.dot(yb_ref[...], wb_ref[1], preferred_element_type=f32)
    merged += _sigmoid(g2_ref[...].astype(f32)) * jnp.dot(yc_ref[...], wb_ref[2], preferred_element_type=f32)
    o_ref[...] = h_ref[...] + jnp.dot(merged.astype(bf), wo_ref[...], preferred_element_type=f32)


def _merge(proj, y_b, y_c, h, conv_w, w_branch, w_out, seq, *, tm):
    t, d = h.shape
    tm = min(tm, seq)
    tiles_per_seq = seq // tm
    r8 = tm // SUBLANES
    nrow8 = t // SUBLANES
    row = lambda w, off: pl.BlockSpec((tm, w), lambda i, _o=off // w: (i, _o))
    prev = lambda off: pl.BlockSpec((SUBLANES, CONV_CH),
                                    lambda i, _o=off // CONV_CH: (jnp.maximum(i * r8 - 1, 0), _o))
    nxt = lambda off: pl.BlockSpec((SUBLANES, CONV_CH),
                                   lambda i, _o=off // CONV_CH: (jnp.minimum((i + 1) * r8, nrow8 - 1), _o))
    full = lambda a: pl.BlockSpec(a.shape, lambda i: (0,) * a.ndim)
    return pl.pallas_call(
        functools.partial(_merge_kernel, tiles_per_seq=tiles_per_seq),
        out_shape=jax.ShapeDtypeStruct((t, d), jnp.float32),
        grid=(t // tm,),
        in_specs=[row(d, OFF_GATES), row(d, OFF_GATES + d), row(d, OFF_GATES + 2 * d),
                  row(CONV_CH, OFF_AB), row(CONV_CH, OFF_AC), row(CONV_CH, OFF_AH),
                  prev(OFF_AC), prev(OFF_AH), nxt(OFF_AC), nxt(OFF_AH),
                  pl.BlockSpec((tm, y_b.shape[1]), lambda i: (i, 0)),
                  pl.BlockSpec((tm, y_c.shape[1]), lambda i: (i, 0)),
                  pl.BlockSpec((tm, d), lambda i: (i, 0)),
                  full(conv_w), full(w_branch), full(w_out)],
        out_specs=pl.BlockSpec((tm, d), lambda i: (i, 0)),
        compiler_params=_cparams("parallel"),
        name="gated_merge",
    )(proj, proj, proj, proj, proj, proj, proj, proj, proj, proj, y_b, y_c, h, conv_w, w_branch, w_out)


def _cross_kernel(h_ref, g_ref, wq_ref, kv_ref, wo_ref, o_ref):
    f32, bf = jnp.float32, jnp.bfloat16
    h = h_ref[...]
    d = h.shape[1]
    hd = d // CROSS_HEADS
    hn = _rms(h, g_ref[...]).astype(bf)
    q = (jnp.dot(hn, wq_ref[...], preferred_element_type=f32) * (hd ** -0.5)).astype(bf)
    outs = []
    for hh in range(CROSS_HEADS):
        kh = kv_ref[:, hh * hd:(hh + 1) * hd]
        vh = kv_ref[:, d + hh * hd:d + (hh + 1) * hd]
        s = lax.dot_general(q[:, hh * hd:(hh + 1) * hd], kh, (((1,), (1,)), ((), ())),
                            preferred_element_type=f32)
        p = jnp.exp(s - jnp.max(s, axis=-1, keepdims=True))
        p = p / jnp.sum(p, axis=-1, keepdims=True)
        outs.append(jnp.dot(p.astype(bf), vh, preferred_element_type=f32).astype(bf))
    o = jnp.concatenate(outs, axis=-1)
    o_ref[...] = h + jnp.dot(o, wo_ref[...], preferred_element_type=f32)


def _cross(h, g, w_cq, memkv, w_co, seq, mem_tokens, *, tm):
    t, d = h.shape
    tm = min(tm, seq)
    tiles_per_seq = seq // tm
    full = lambda a: pl.BlockSpec(a.shape, lambda i: (0,) * a.ndim)
    return pl.pallas_call(
        _cross_kernel,
        out_shape=jax.ShapeDtypeStruct((t, d), jnp.float32),
        grid=(t // tm,),
        in_specs=[pl.BlockSpec((tm, d), lambda i: (i, 0)), full(g), full(w_cq),
                  pl.BlockSpec((mem_tokens, 2 * d), lambda i: (i // tiles_per_seq, 0)), full(w_co)],
        out_specs=pl.BlockSpec((tm, d), lambda i: (i, 0)),
        compiler_params=_cparams("parallel"),
        name="cross_attn",
    )(h, g, w_cq, memkv, w_co)


def _topk_rows(sc, k, vals_ref, idx_ref, payload=None):
    n = sc.shape[0]
    row = lax.broadcasted_iota(jnp.int32, sc.shape, 0).astype(jnp.float32)
    for r in range(k):
        m = jnp.max(sc, axis=0, keepdims=True)
        first = jnp.min(jnp.where(sc == m, row, float(n)), axis=0, keepdims=True)
        hit = row == first
        vals_ref[r:r + 1, :] = m
        if payload is None:
            idx_ref[r:r + 1, :] = first
        else:
            idx_ref[r:r + 1, :] = jnp.sum(jnp.where(hit, payload, 0.0), axis=0, keepdims=True)
        sc = jnp.where(hit, -jnp.inf, sc)


def _route_kernel(h_ref, g_ref, wq_ref, keys_ref, xn_o, idx_o, gate_o,
                  s1_ref, i1_ref, s2_ref, i2_ref, cs_ref, ci_ref, ts_ref, ti_ref):
    f32, bf = jnp.float32, jnp.bfloat16
    xn = _rms(h_ref[...], g_ref[...])
    xn_o[...] = xn
    q = jnp.dot(xn.astype(bf), wq_ref[...], preferred_element_type=f32).astype(bf)
    kd = PEER_KEY_DIM
    for hh in range(PEER_HEADS):
        for p, (s_ref, i_ref) in enumerate(((s1_ref, i1_ref), (s2_ref, i2_ref))):
            c0 = (hh * 2 + p) * kd
            sc = lax.dot_general(keys_ref[hh, p], q[:, c0:c0 + kd], (((1,), (1,)), ((), ())),
                                 preferred_element_type=f32)
            _topk_rows(sc, PEER_TOPK, s_ref, i_ref)
        nk = float(PEER_N_KEYS)
        cs_ref[0:PEER_TOPK, :] = s1_ref[0:1, :] + s2_ref[...]
        ci_ref[0:PEER_TOPK, :] = i1_ref[0:1, :] * nk + i2_ref[...]
        off = PEER_TOPK
        for a in range(1, PEER_TOPK // 2):
            cs_ref[off:off + SUBLANES, :] = s1_ref[a:a + 1, :] + s2_ref[0:SUBLANES, :]
            ci_ref[off:off + SUBLANES, :] = i1_ref[a:a + 1, :] * nk + i2_ref[0:SUBLANES, :]
            off += SUBLANES
        cs_ref[off:off + SUBLANES, :] = s1_ref[PEER_TOPK // 2:PEER_TOPK, :] + s2_ref[0:1, :]
        ci_ref[off:off + SUBLANES, :] = i1_ref[PEER_TOPK // 2:PEER_TOPK, :] * nk + i2_ref[0:1, :]
        _topk_rows(cs_ref[...], PEER_TOPK, ts_ref, ti_ref, payload=ci_ref[...])
        ts = ts_ref[...]
        e = jnp.exp(ts - ts[0:1, :])
        gate_o[hh * PEER_TOPK:(hh + 1) * PEER_TOPK, :] = e / jnp.sum(e, axis=0, keepdims=True)
        idx_o[hh * PEER_TOPK:(hh + 1) * PEER_TOPK, :] = ti_ref[...].astype(jnp.int32)


def _route(h, g, w_q, keys, *, tm):
    t, d = h.shape
    tm = min(tm, t)
    nj = PEER_HEADS * PEER_TOPK
    full = lambda a: pl.BlockSpec(a.shape, lambda i: (0,) * a.ndim)
    assert PEER_TOPK == 2 * SUBLANES
    n_cand = PEER_TOPK + (PEER_TOPK // 2) * SUBLANES
    small = lambda: pltpu.VMEM((PEER_TOPK, tm), jnp.float32)
    big = lambda: pltpu.VMEM((n_cand, tm), jnp.float32)
    return pl.pallas_call(
        _route_kernel,
        out_shape=[jax.ShapeDtypeStruct((t, d), jnp.float32),
                   jax.ShapeDtypeStruct((nj, t), jnp.int32),
                   jax.ShapeDtypeStruct((nj, t), jnp.float32)],
        grid=(t // tm,),
        in_specs=[pl.BlockSpec((tm, d), lambda i: (i, 0)), full(g), full(w_q), full(keys)],
        out_specs=[pl.BlockSpec((tm, d), lambda i: (i, 0)),
                   pl.BlockSpec((nj, tm), lambda i: (0, i)),
                   pl.BlockSpec((nj, tm), lambda i: (0, i))],
        scratch_shapes=[small(), small(), small(), small(), big(), big(), small(), small()],
        compiler_params=_cparams("parallel"),
        name="peer_route",
    )(h, g, w_q, keys)


def _gelu(x):
    return 0.5 * x * (1.0 + lax.erf(x * (2.0 ** -0.5)))


def _experts_kernel(idx_ref, idxn_ref, gate_ref, xn_ref, h_ref, uv_hbm, o_ref, buf, sem):
    f32, bf = jnp.float32, jnp.bfloat16
    tt = xn_ref.shape[0] // 2
    d = xn_ref.shape[1]
    nj = gate_ref.shape[1]
    nrows = tt * nj
    nchunk = d // LANES
    ngroups = nrows // SUBLANES
    i = pl.program_id(0)
    n = pl.num_programs(0)

    def start_group(iref, base, sl, g):
        for k in range(SUBLANES):
            e = iref[0, 0, base + g * SUBLANES + k]
            pltpu.make_async_copy(uv_hbm.at[e], buf.at[sl, g, :, k, :], sem.at[sl]).start(priority=k % 2)

    def wait_slot(sl):
        pltpu.make_async_copy(buf.at[sl], buf.at[sl], sem.at[sl]).wait()

    @pl.when(i == 0)
    def _():
        def body(g, carry):
            start_group(idx_ref, 0, 0, g)
            return carry
        lax.fori_loop(0, ngroups, body, 0)

    issue_phases = (2 * nchunk * 3) // 4
    per_phase = -(-ngroups // issue_phases)
    col = lax.broadcasted_iota(jnp.int32, (tt, nrows), 1)
    row = lax.broadcasted_iota(jnp.int32, (tt, nrows), 0)
    own = (col >= row * nj) & (col < (row + 1) * nj)

    def block(sl, next_iref, next_base):
        tok = slice(sl * tt, (sl + 1) * tt)

        def start_phase(ph):
            for g in range(ph * per_phase, min((ph + 1) * per_phase, ngroups)):
                start_group(next_iref, next_base, 1 - sl, g)

        def words(c):
            return buf[sl, :, c, :, :].reshape(nrows, LANES)

        wait_slot(sl)
        xn = xn_ref[tok, :].astype(bf)
        act = jnp.zeros((tt, nrows), f32)
        for c in range(nchunk):
            uc = lax.bitcast_convert_type(words(c) & jnp.uint32(0xFFFF0000), f32).astype(bf)
            act += lax.dot_general(xn[:, c * LANES:(c + 1) * LANES], uc, (((1,), (1,)), ((), ())),
                                   preferred_element_type=f32)
            start_phase(c)
        gate = jnp.concatenate([gate_ref[tok, :]] * tt, axis=1)
        w = jnp.where(own, gate * _gelu(act), 0.0).astype(bf)
        for c in range(nchunk):
            vc = lax.bitcast_convert_type(words(c) << 16, f32).astype(bf)
            cs = slice(c * LANES, (c + 1) * LANES)
            o_ref[tok, cs] = h_ref[tok, cs] + jnp.dot(w, vc, preferred_element_type=f32)
            start_phase(nchunk + c)

    block(0, idx_ref, nrows)
    block(1, idxn_ref, 0)

    @pl.when(i == n - 1)
    def _():
        wait_slot(0)


def _experts(idx, gate, xn, h, uv, *, tt):
    t, d = h.shape
    nj = gate.shape[1]
    n = t // (2 * tt)
    nrows = tt * nj
    idx_step = idx.reshape(n, 1, 2 * nrows)
    idx_block = idx.reshape(2 * n, 1, nrows)
    return pl.pallas_call(
        _experts_kernel,
        out_shape=jax.ShapeDtypeStruct((t, d), jnp.float32),
        grid=(n,),
        in_specs=[pl.BlockSpec((1, 1, 2 * nrows), lambda i: (i, 0, 0), memory_space=pltpu.SMEM),
                  pl.BlockSpec((1, 1, nrows), lambda i: (jnp.minimum(2 * i + 2, 2 * n - 2), 0, 0),
                               memory_space=pltpu.SMEM),
                  pl.BlockSpec((2 * tt, nj), lambda i: (i, 0)),
                  pl.BlockSpec((2 * tt, d), lambda i: (i, 0)),
                  pl.BlockSpec((2 * tt, d), lambda i: (i, 0)),
                  pl.BlockSpec(memory_space=pl.ANY)],
        out_specs=pl.BlockSpec((2 * tt, d), lambda i: (i, 0)),
        scratch_shapes=[pltpu.VMEM((2, nrows // SUBLANES, d // LANES, SUBLANES, LANES), jnp.uint32),
                        pltpu.SemaphoreType.DMA((2,))],
        compiler_params=_cparams("arbitrary"),
        name="peer_experts",
    )(idx_step, idx_block, gate, xn, h, uv)


def _final_norm_kernel(x_ref, g_ref, o_ref):
    o_ref[...] = _rms(x_ref[...], g_ref[...])


def _final_norm(h, g, *, tm):
    t, d = h.shape
    tm = min(tm, t)
    return pl.pallas_call(
        _final_norm_kernel,
        out_shape=jax.ShapeDtypeStruct((t, d), jnp.float32),
        grid=(t // tm,),
        in_specs=[pl.BlockSpec((tm, d), lambda i: (i, 0)), pl.BlockSpec((1, d), lambda i: (0, 0))],
        out_specs=pl.BlockSpec((tm, d), lambda i: (i, 0)),
        compiler_params=_cparams("parallel"),
        name="final_norm",
    )(h, g.reshape(1, d))


def _rope_lane_tables(positions, rot_dim, period, lane0):
    half = rot_dim // 2
    inv = 1.0 / (ROPE_THETA ** (jnp.arange(0, rot_dim, 2, dtype=jnp.float32) / rot_dim))
    ang = positions.astype(jnp.float32).reshape(-1, 1) * inv[None, :]
    cos, sin = jnp.cos(ang), jnp.sin(ang)
    lane = np.arange(LANES) % period - lane0
    in_rot = (lane >= 0) & (lane < rot_dim)
    sel = np.where(in_rot, lane % half, 0)
    sign = np.where(lane < half, -1.0, 1.0).astype(np.float32)
    c = jnp.where(in_rot[None, :], cos[:, sel], 1.0)
    s = jnp.where(in_rot[None, :], sin[:, sel] * sign[None, :], 0.0)
    return c, s


def _layer_weights(l, w_in, mla_w_uq, mla_w_ukv):
    bf = jnp.bfloat16
    d = w_in.shape[1]
    offs = np.cumsum([0, CONV_CH, CONV_CH, CONV_CH, 512, 512, 512, MLA_Q_RANK, MLA_KV_RANK, MLA_ROPE])
    w = w_in[l]
    seg = lambda i: w[:, int(offs[i]):int(offs[i + 1])]
    gates = w[:, int(offs[9]):]
    kr = jnp.pad(seg(8), ((0, 0), (0, LANES - MLA_ROPE)))
    w_in_p = jnp.concatenate([gates, seg(0), seg(1), seg(2), seg(3), seg(4), seg(5), seg(6), seg(7), kr],
                             axis=1).astype(bf)
    qd = MLA_NOPE + MLA_ROPE
    wuq = mla_w_uq[l].reshape(MLA_Q_RANK, MLA_HEADS, qd)
    wuq = jnp.pad(wuq, ((0, 0), (0, 0), (0, HEAD_W - qd))).reshape(MLA_Q_RANK, MLA_HEADS * HEAD_W).astype(bf)
    wukv = mla_w_ukv[l].reshape(MLA_KV_RANK, MLA_HEADS, MLA_NOPE + MLA_V)
    wk = jnp.pad(wukv[:, :, :MLA_NOPE], ((0, 0), (0, 0), (0, HEAD_W - MLA_NOPE)))
    wk = wk.reshape(MLA_KV_RANK, MLA_HEADS * HEAD_W).astype(bf)
    wv = wukv[:, :, MLA_NOPE:].reshape(MLA_KV_RANK, MLA_HEADS * MLA_V).astype(bf)
    return w_in_p, wuq, wk, wv


def _pack_tables(u, v):
    hi = lax.bitcast_convert_type(u.astype(jnp.bfloat16), jnp.uint16).astype(jnp.uint32)
    lo = lax.bitcast_convert_type(v.astype(jnp.bfloat16), jnp.uint16).astype(jnp.uint32)
    return ((hi << 16) | lo).reshape(u.shape[0], u.shape[1] // LANES, LANES)


def _placement():
    e = np.zeros((LANES, MLA_HEADS * HEAD_W), np.float32)
    for h in range(MLA_HEADS):
        for c in range(MLA_ROPE):
            e[c, h * HEAD_W + MLA_NOPE + c] = 1.0
    return jnp.asarray(e, jnp.bfloat16)


def kernel(x, mem, positions, norm_mix, w_in, conv_w, diff_lambda, diff_subln, mla_q_norm, mla_w_uq,
           mla_kv_norm, mla_w_ukv, w_branch, w_out, norm_cross, norm_mem, w_cq, w_ckv, w_co, norm_ffn,
           peer_w_q, peer_keys, peer_u, peer_v, final_norm):
    bf = jnp.bfloat16
    batch, seq, d = x.shape
    mem_tokens = mem.shape[1]
    depth = w_in.shape[0]
    t = batch * seq
    h = x.reshape(t, d)
    mem2 = mem.reshape(batch * mem_tokens, d)

    cd, sd = _rope_lane_tables(positions, DIFF_ROT, DIFF_HEAD_DIM, 0)
    cm, sm = _rope_lane_tables(positions, MLA_ROPE, LANES, MLA_NOPE)
    place = _placement()
    tt = 16

    for l in range(depth):
        lam_init = 0.8 - 0.6 * math.exp(-0.3 * l)
        w_in_p, wuq, wk, wv = _layer_weights(l, w_in, mla_w_uq, mla_w_ukv)

        proj = _norm_matmul(h, norm_mix[l], w_in_p, tm=512, tn=PROJ_W // 2, name="in_proj")
        dq, dk, mq, mk, mv = _prep(proj, cd, sd, cm, sm, mla_q_norm[l].reshape(1, -1), wuq,
                                   mla_kv_norm[l].reshape(1, -1), wk, wv, place, tm=512)
        y_b = _flash(dq, dk, proj, OFF_DV, batch, seq, DIFF_HEADS, n_maps=2, tq=ATTN_TQ, tk=ATTN_TK,
                     lam_init=lam_init, lamv=diff_lambda[l], subln=diff_subln[l].reshape(1, -1),
                     name="diff_attn")
        y_c = _flash(mq, mk, mv, 0, batch, seq, MLA_HEADS, n_maps=1, tq=ATTN_TQ, tk=ATTN_TK, name="mla_attn")
        h = _merge(proj, y_b, y_c, h, conv_w[l], w_branch[l].astype(bf), w_out[l].astype(bf), seq, tm=256)

        memkv = _norm_matmul(mem2, norm_mem[l], w_ckv[l].astype(bf), tm=256, tn=512, name="mem_kv")
        h = _cross(h, norm_cross[l].reshape(1, d), w_cq[l].astype(bf), memkv, w_co[l].astype(bf),
                   seq, mem_tokens, tm=256)

        xn, idx, gate = _route(h, norm_ffn[l].reshape(1, d), peer_w_q[l].astype(bf),
                               peer_keys[l].astype(bf), tm=256)
        uv = _pack_tables(peer_u[l], peer_v[l])
        h = _experts(idx.T, gate.T, xn, h, uv, tt=tt)

    out = _final_norm(h, final_norm, tm=512)
    return out.reshape(batch, seq, d)
```

```python
import functools
import math

import numpy as np
import jax
import jax.numpy as jnp
from jax import lax
from jax.experimental import pallas as pl
from jax.experimental.pallas import tpu as pltpu

EPS = 1e-6
ROPE_THETA = 500000.0
CONV_CH = 512
DIFF_HEADS = 4
DIFF_HEAD_DIM = 64
DIFF_ROT = DIFF_HEAD_DIM // 4
MLA_HEADS = 4
MLA_Q_RANK = 256
MLA_KV_RANK = 128
MLA_NOPE = 64
MLA_ROPE = 32
MLA_V = 128
N_BRANCH = 3
CROSS_HEADS = 4
PEER_HEADS = 8
PEER_N_KEYS = 128
PEER_KEY_DIM = 128
PEER_TOPK = 16

LANES = 128
SUBLANES = 8
VMEM_LIMIT_BYTES = 56 * 1024 * 1024

ATTN_TQ = 512
ATTN_TK = 8192

HEAD_W = 128
OFF_GATES = 0
OFF_AB = 3072
OFF_AC = OFF_AB + CONV_CH
OFF_AH = OFF_AC + CONV_CH
OFF_DQ = OFF_AH + CONV_CH
OFF_DK = OFF_DQ + 512
OFF_DV = OFF_DK + 512
OFF_CQ = OFF_DV + 512
OFF_CKV = OFF_CQ + MLA_Q_RANK
OFF_KR = OFF_CKV + MLA_KV_RANK
PROJ_W = OFF_KR + LANES


def _cparams(*sem):
    return pltpu.CompilerParams(dimension_semantics=sem, vmem_limit_bytes=VMEM_LIMIT_BYTES)


def _norm_matmul_kernel(x_ref, g_ref, w_ref, o_ref, xn_ref):
    @pl.when(pl.program_id(1) == 0)
    def _():
        x = x_ref[...]
        ms = jnp.mean(x * x, axis=-1, keepdims=True)
        xn_ref[...] = (x * lax.rsqrt(ms + EPS) * g_ref[...]).astype(xn_ref.dtype)

    o_ref[...] = jnp.dot(xn_ref[...], w_ref[...],
                         preferred_element_type=jnp.float32).astype(o_ref.dtype)


def _norm_matmul(x, g, w, *, tm, tn, name):
    t, k = x.shape
    n = w.shape[1]
    tm = min(tm, t)
    return pl.pallas_call(
        _norm_matmul_kernel,
        out_shape=jax.ShapeDtypeStruct((t, n), jnp.bfloat16),
        grid=(t // tm, n // tn),
        in_specs=[pl.BlockSpec((tm, k), lambda i, j: (i, 0)),
                  pl.BlockSpec((1, k), lambda i, j: (0, 0)),
                  pl.BlockSpec((k, tn), lambda i, j: (0, j))],
        out_specs=pl.BlockSpec((tm, tn), lambda i, j: (i, j)),
        scratch_shapes=[pltpu.VMEM((tm, k), jnp.bfloat16)],
        compiler_params=_cparams("parallel", "arbitrary"),
        name=name,
    )(x, g.reshape(1, k), w)


def _rope_block(xb, c, s, lo_mask, shift):
    partner = jnp.where(lo_mask, pltpu.roll(xb, LANES - shift, axis=1), pltpu.roll(xb, shift, axis=1))
    return xb * c + partner * s


def _rms(x, g):
    ms = jnp.mean(x * x, axis=-1, keepdims=True)
    return x * lax.rsqrt(ms + EPS) * g


def _prep_kernel(dq_ref, dk_ref, cq_ref, ckv_ref, kr_ref, cd_ref, sd_ref, cm_ref, sm_ref,
                 qn_ref, wuq_ref, kvn_ref, wk_ref, wv_ref, place_ref,
                 dq_o, dk_o, mq_o, mk_o, mv_o):
    f32 = jnp.float32
    tm = dq_ref.shape[0]
    lane = lax.broadcasted_iota(jnp.int32, (tm, LANES), 1)
    cd, sd, cm, sm = cd_ref[...], sd_ref[...], cm_ref[...], sm_ref[...]
    d_lo = (lane % DIFF_HEAD_DIM) < (DIFF_ROT // 2)
    m_lo = lane < (MLA_NOPE + MLA_ROPE // 2)
    d_scale = DIFF_HEAD_DIM ** -0.5
    m_scale = (MLA_NOPE + MLA_ROPE) ** -0.5
    first_half = lane < DIFF_HEAD_DIM

    for h in range(DIFF_HEADS):
        sl = slice(h * HEAD_W, (h + 1) * HEAD_W)
        qb = _rope_block(dq_ref[:, sl].astype(f32), cd, sd, d_lo, DIFF_ROT // 2) * d_scale
        kb = _rope_block(dk_ref[:, sl].astype(f32), cd, sd, d_lo, DIFF_ROT // 2)
        dq_o[:, 2 * h * HEAD_W:(2 * h + 1) * HEAD_W] = jnp.where(first_half, qb, 0.0).astype(dq_o.dtype)
        dq_o[:, (2 * h + 1) * HEAD_W:(2 * h + 2) * HEAD_W] = jnp.where(first_half, 0.0, qb).astype(dq_o.dtype)
        dk_o[:, sl] = kb.astype(dk_o.dtype)

    cqn = _rms(cq_ref[...].astype(f32), qn_ref[...]).astype(jnp.bfloat16)
    q = jnp.dot(cqn, wuq_ref[...], preferred_element_type=f32)
    ckvn = _rms(ckv_ref[...].astype(f32), kvn_ref[...]).astype(jnp.bfloat16)
    k = jnp.dot(ckvn, wk_ref[...], preferred_element_type=f32)
    k = k + jnp.dot(kr_ref[...], place_ref[...], preferred_element_type=f32)
    v = jnp.dot(ckvn, wv_ref[...], preferred_element_type=f32)
    for h in range(MLA_HEADS):
        sl = slice(h * HEAD_W, (h + 1) * HEAD_W)
        mq_o[:, sl] = (_rope_block(q[:, sl], cm, sm, m_lo, MLA_ROPE // 2) * m_scale).astype(mq_o.dtype)
        mk_o[:, sl] = _rope_block(k[:, sl], cm, sm, m_lo, MLA_ROPE // 2).astype(mk_o.dtype)
    mv_o[...] = v.astype(mv_o.dtype)


def _prep(proj, cd, sd, cm, sm, qn, wuq, kvn, wk, wv, place, *, tm):
    t = proj.shape[0]
    tm = min(tm, t)
    row = lambda w, off: pl.BlockSpec((tm, w), lambda i, _o=off // w: (i, _o))
    full = lambda a: pl.BlockSpec(a.shape, lambda i: (0,) * a.ndim)
    tab = pl.BlockSpec((tm, LANES), lambda i: (i, 0))
    bf = jnp.bfloat16
    return pl.pallas_call(
        _prep_kernel,
        out_shape=[jax.ShapeDtypeStruct((t, 2 * DIFF_HEADS * HEAD_W), bf),
                   jax.ShapeDtypeStruct((t, DIFF_HEADS * HEAD_W), bf),
                   jax.ShapeDtypeStruct((t, MLA_HEADS * HEAD_W), bf),
                   jax.ShapeDtypeStruct((t, MLA_HEADS * HEAD_W), bf),
                   jax.ShapeDtypeStruct((t, MLA_HEADS * HEAD_W), bf)],
        grid=(t // tm,),
        in_specs=[row(512, OFF_DQ), row(512, OFF_DK), row(MLA_Q_RANK, OFF_CQ),
                  row(MLA_KV_RANK, OFF_CKV), row(LANES, OFF_KR), tab, tab, tab, tab,
                  full(qn), full(wuq), full(kvn), full(wk), full(wv), full(place)],
        out_specs=[pl.BlockSpec((tm, 2 * DIFF_HEADS * HEAD_W), lambda i: (i, 0)),
                   pl.BlockSpec((tm, DIFF_HEADS * HEAD_W), lambda i: (i, 0)),
                   pl.BlockSpec((tm, MLA_HEADS * HEAD_W), lambda i: (i, 0)),
                   pl.BlockSpec((tm, MLA_HEADS * HEAD_W), lambda i: (i, 0)),
                   pl.BlockSpec((tm, MLA_HEADS * HEAD_W), lambda i: (i, 0))],
        compiler_params=_cparams("parallel"),
        name="attn_prep",
    )(proj, proj, proj, proj, proj, cd, sd, cm, sm, qn, wuq, kvn, wk, wv, place)


def _flash_kernel(*refs, n_maps, tk, lam_init):
    if n_maps == 2:
        q_ref, k_ref, v_ref, lamv_ref, subln_ref, o_ref, m_ref, l_ref, acc_ref = refs
    else:
        q_ref, k_ref, v_ref, o_ref, m_ref, l_ref, acc_ref = refs
    f32 = jnp.float32
    tq = q_ref.shape[0]
    nk = k_ref.shape[0] // tk
    m_ref[...] = jnp.full(m_ref.shape, -jnp.inf, f32)
    l_ref[...] = jnp.zeros(l_ref.shape, f32)
    acc_ref[...] = jnp.zeros(acc_ref.shape, f32)

    def body(c, carry):
        start = pl.multiple_of(c * tk, tk)
        kc = k_ref[pl.ds(start, tk), :]
        vc = v_ref[pl.ds(start, tk), :]
        for mi in range(n_maps):
            q = q_ref[:, mi * HEAD_W:(mi + 1) * HEAD_W]
            s = lax.dot_general(q, kc, (((1,), (1,)), ((), ())), preferred_element_type=f32)
            m_prev = m_ref[mi]
            m_new = jnp.maximum(m_prev, jnp.max(s, axis=-1, keepdims=True))
            alpha = jnp.exp(m_prev - m_new)
            p = jnp.exp(s - m_new[:, :1])
            l_ref[mi] = alpha * l_ref[mi] + jnp.sum(p, axis=-1, keepdims=True)
            acc_ref[mi] = alpha * acc_ref[mi] + jnp.dot(p.astype(vc.dtype), vc, preferred_element_type=f32)
            m_ref[mi] = m_new
        return carry

    lax.fori_loop(0, nk, body, 0)

    if n_maps == 2:
        lv = lamv_ref[...]
        lam = (jnp.exp(jnp.sum(lv[0:1] * lv[1:2], axis=-1, keepdims=True))
               - jnp.exp(jnp.sum(lv[2:3] * lv[3:4], axis=-1, keepdims=True)) + lam_init)
        o = acc_ref[0] / l_ref[0] - lam * (acc_ref[1] / l_ref[1])
        o = _rms(o, subln_ref[...]) * (1.0 - lam_init)
    else:
        o = acc_ref[0] / l_ref[0]
    o_ref[...] = o.astype(o_ref.dtype)


def _flash(q, k, v, v_col0, batch, seq, heads, *, n_maps, tq, tk, lam_init=0.0, lamv=None, subln=None, name):
    tq = min(tq, seq)
    tk = min(tk, seq)
    nq = seq // tq
    vb = v_col0 // HEAD_W
    in_specs = [pl.BlockSpec((tq, n_maps * HEAD_W), lambda b, h, i: (b * nq + i, h)),
                pl.BlockSpec((seq, HEAD_W), lambda b, h, i: (b, h)),
                pl.BlockSpec((seq, HEAD_W), lambda b, h, i: (b, vb + h))]
    args = [q, k, v]
    if n_maps == 2:
        in_specs += [pl.BlockSpec(lamv.shape, lambda b, h, i: (0, 0)),
                     pl.BlockSpec(subln.shape, lambda b, h, i: (0, 0))]
        args += [lamv, subln]
    return pl.pallas_call(
        functools.partial(_flash_kernel, n_maps=n_maps, tk=tk, lam_init=lam_init),
        out_shape=jax.ShapeDtypeStruct((batch * seq, heads * HEAD_W), jnp.bfloat16),
        grid=(batch, heads, nq),
        in_specs=in_specs,
        out_specs=pl.BlockSpec((tq, HEAD_W), lambda b, h, i: (b * nq + i, h)),
        scratch_shapes=[pltpu.VMEM((n_maps, tq, HEAD_W), jnp.float32),
                        pltpu.VMEM((n_maps, tq, HEAD_W), jnp.float32),
                        pltpu.VMEM((n_maps, tq, HEAD_W), jnp.float32)],
        compiler_params=_cparams("parallel", "parallel", "arbitrary"),
        name=name,
    )(*args)


def _sigmoid(x):
    return 1.0 / (1.0 + jnp.exp(-x))


def _merge_kernel(g0_ref, g1_ref, g2_ref, ab_ref, ac_ref, ah_ref, pc_ref, ph_ref, nc_ref, nh_ref,
                  yb_ref, yc_ref, h_ref, cw_ref, wb_ref, wo_ref, o_ref, *, tiles_per_seq):
    f32 = jnp.float32
    tm = ab_ref.shape[0]
    i = pl.program_id(0)
    pos = i % tiles_per_seq
    z = ac_ref[...].astype(f32) * ah_ref[...].astype(f32)
    zp_edge = pc_ref[SUBLANES - 1:SUBLANES, :].astype(f32) * ph_ref[SUBLANES - 1:SUBLANES, :].astype(f32)
    zn_edge = nc_ref[0:1, :].astype(f32) * nh_ref[0:1, :].astype(f32)
    zp_edge = jnp.where(pos > 0, zp_edge, 0.0)
    zn_edge = jnp.where(pos < tiles_per_seq - 1, zn_edge, 0.0)
    row = lax.broadcasted_iota(jnp.int32, z.shape, 0)
    zp = jnp.where(row == 0, zp_edge, pltpu.roll(z, 1, axis=0))
    zn = jnp.where(row == tm - 1, zn_edge, pltpu.roll(z, tm - 1, axis=0))
    cw = cw_ref[...]
    y_a = ab_ref[...].astype(f32) * (cw[0:1] * zp + cw[1:2] * z + cw[2:3] * zn)

    bf = jnp.bfloat16
    merged = _sigmoid(g0_ref[...].astype(f32)) * jnp.dot(y_a.astype(bf), wb_ref[0], preferred_element_type=f32)
    merged += _sigmoid(g1_ref[...].astype(f32)) * jnp.dot(yb_ref[...], wb_ref[1], preferred_element_type=f32)
    merged += _sigmoid(g2_ref[...].astype(f32)) * jnp.dot(yc_ref[...], wb_ref[2], preferred_element_type=f32)
    o_ref[...] = h_ref[...] + jnp.dot(merged.astype(bf), wo_ref[...], preferred_element_type=f32)


def _merge(proj, y_b, y_c, h, conv_w, w_branch, w_out, seq, *, tm):
    t, d = h.shape
    tm = min(tm, seq)
    tiles_per_seq = seq // tm
    r8 = tm // SUBLANES
    nrow8 = t // SUBLANES
    row = lambda w, off: pl.BlockSpec((tm, w), lambda i, _o=off // w: (i, _o))
    prev = lambda off: pl.BlockSpec((SUBLANES, CONV_CH),
                                    lambda i, _o=off // CONV_CH: (jnp.maximum(i * r8 - 1, 0), _o))
    nxt = lambda off: pl.BlockSpec((SUBLANES, CONV_CH),
                                   lambda i, _o=off // CONV_CH: (jnp.minimum((i + 1) * r8, nrow8 - 1), _o))
    full = lambda a: pl.BlockSpec(a.shape, lambda i: (0,) * a.ndim)
    return pl.pallas_call(
        functools.partial(_merge_kernel, tiles_per_seq=tiles_per_seq),
        out_shape=jax.ShapeDtypeStruct((t, d), jnp.float32),
        grid=(t // tm,),
        in_specs=[row(d, OFF_GATES), row(d, OFF_GATES + d), row(d, OFF_GATES + 2 * d),
                  row(CONV_CH, OFF_AB), row(CONV_CH, OFF_AC), row(CONV_CH, OFF_AH),
                  prev(OFF_AC), prev(OFF_AH), nxt(OFF_AC), nxt(OFF_AH),
                  pl.BlockSpec((tm, y_b.shape[1]), lambda i: (i, 0)),
                  pl.BlockSpec((tm, y_c.shape[1]), lambda i: (i, 0)),
                  pl.BlockSpec((tm, d), lambda i: (i, 0)),
                  full(conv_w), full(w_branch), full(w_out)],
        out_specs=pl.BlockSpec((tm, d), lambda i: (i, 0)),
        compiler_params=_cparams("parallel"),
        name="gated_merge",
    )(proj, proj, proj, proj, proj, proj, proj, proj, proj, proj, y_b, y_c, h, conv_w, w_branch, w_out)


def _cross_kernel(h_ref, g_ref, wq_ref, kv_ref, wo_ref, o_ref):
    f32, bf = jnp.float32, jnp.bfloat16
    h = h_ref[...]
    d = h.shape[1]
    hd = d // CROSS_HEADS
    hn = _rms(h, g_ref[...]).astype(bf)
    q = (jnp.dot(hn, wq_ref[...], preferred_element_type=f32) * (hd ** -0.5)).astype(bf)
    outs = []
    for hh in range(CROSS_HEADS):
        kh = kv_ref[:, hh * hd:(hh + 1) * hd]
        vh = kv_ref[:, d + hh * hd:d + (hh + 1) * hd]
        s = lax.dot_general(q[:, hh * hd:(hh + 1) * hd], kh, (((1,), (1,)), ((), ())),
                            preferred_element_type=f32)
        p = jnp.exp(s - jnp.max(s, axis=-1, keepdims=True))
        p = p / jnp.sum(p, axis=-1, keepdims=True)
        outs.append(jnp.dot(p.astype(bf), vh, preferred_element_type=f32).astype(bf))
    o = jnp.concatenate(outs, axis=-1)
    o_ref[...] = h + jnp.dot(o, wo_ref[...], preferred_element_type=f32)


def _cross(h, g, w_cq, memkv, w_co, seq, mem_tokens, *, tm):
    t, d = h.shape
    tm = min(tm, seq)
    tiles_per_seq = seq // tm
    full = lambda a: pl.BlockSpec(a.shape, lambda i: (0,) * a.ndim)
    return pl.pallas_call(
        _cross_kernel,
        out_shape=jax.ShapeDtypeStruct((t, d), jnp.float32),
        grid=(t // tm,),
        in_specs=[pl.BlockSpec((tm, d), lambda i: (i, 0)), full(g), full(w_cq),
                  pl.BlockSpec((mem_tokens, 2 * d), lambda i: (i // tiles_per_seq, 0)), full(w_co)],
        out_specs=pl.BlockSpec((tm, d), lambda i: (i, 0)),
        compiler_params=_cparams("parallel"),
        name="cross_attn",
    )(h, g, w_cq, memkv, w_co)


def _topk_rows(sc, k, vals_ref, idx_ref, payload=None):
    n = sc.shape[0]
    row = lax.broadcasted_iota(jnp.int32, sc.shape, 0).astype(jnp.float32)
    for r in range(k):
        m = jnp.max(sc, axis=0, keepdims=True)
        first = jnp.min(jnp.where(sc == m, row, float(n)), axis=0, keepdims=True)
        hit = row == first
        vals_ref[r:r + 1, :] = m
        if payload is None:
            idx_ref[r:r + 1, :] = first
        else:
            idx_ref[r:r + 1, :] = jnp.sum(jnp.where(hit, payload, 0.0), axis=0, keepdims=True)
        sc = jnp.where(hit, -jnp.inf, sc)


def _route_kernel(h_ref, g_ref, wq_ref, keys_ref, xn_o, idx_o, gate_o,
                  s1_ref, i1_ref, s2_ref, i2_ref, cs_ref, ci_ref, ts_ref, ti_ref):
    f32, bf = jnp.float32, jnp.bfloat16
    xn = _rms(h_ref[...], g_ref[...])
    xn_o[...] = xn
    q = jnp.dot(xn.astype(bf), wq_ref[...], preferred_element_type=f32).astype(bf)
    kd = PEER_KEY_DIM
    for hh in range(PEER_HEADS):
        for p, (s_ref, i_ref) in enumerate(((s1_ref, i1_ref), (s2_ref, i2_ref))):
            c0 = (hh * 2 + p) * kd
            sc = lax.dot_general(keys_ref[hh, p], q[:, c0:c0 + kd], (((1,), (1,)), ((), ())),
                                 preferred_element_type=f32)
            _topk_rows(sc, PEER_TOPK, s_ref, i_ref)
        nk = float(PEER_N_KEYS)
        cs_ref[0:PEER_TOPK, :] = s1_ref[0:1, :] + s2_ref[...]
        ci_ref[0:PEER_TOPK, :] = i1_ref[0:1, :] * nk + i2_ref[...]
        off = PEER_TOPK
        for a in range(1, PEER_TOPK // 2):
            cs_ref[off:off + SUBLANES, :] = s1_ref[a:a + 1, :] + s2_ref[0:SUBLANES, :]
            ci_ref[off:off + SUBLANES, :] = i1_ref[a:a + 1, :] * nk + i2_ref[0:SUBLANES, :]
            off += SUBLANES
        cs_ref[off:off + SUBLANES, :] = s1_ref[PEER_TOPK // 2:PEER_TOPK, :] + s2_ref[0:1, :]
        ci_ref[off:off + SUBLANES, :] = i1_ref[PEER_TOPK // 2:PEER_TOPK, :] * nk + i2_ref[0:1, :]
        _topk_rows(cs_ref[...], PEER_TOPK, ts_ref, ti_ref, payload=ci_ref[...])
        ts = ts_ref[...]
        e = jnp.exp(ts - ts[0:1, :])
        gate_o[hh * PEER_TOPK:(hh + 1) * PEER_TOPK, :] = e / jnp.sum(e, axis=0, keepdims=True)
        idx_o[hh * PEER_TOPK:(hh + 1) * PEER_TOPK, :] = ti_ref[...].astype(jnp.int32)


def _route(h, g, w_q, keys, *, tm):
    t, d = h.shape
    tm = min(tm, t)
    nj = PEER_HEADS * PEER_TOPK
    full = lambda a: pl.BlockSpec(a.shape, lambda i: (0,) * a.ndim)
    assert PEER_TOPK == 2 * SUBLANES
    n_cand = PEER_TOPK + (PEER_TOPK // 2) * SUBLANES
    small = lambda: pltpu.VMEM((PEER_TOPK, tm), jnp.float32)
    big = lambda: pltpu.VMEM((n_cand, tm), jnp.float32)
    return pl.pallas_call(
        _route_kernel,
        out_shape=[jax.ShapeDtypeStruct((t, d), jnp.float32),
                   jax.ShapeDtypeStruct((nj, t), jnp.int32),
                   jax.ShapeDtypeStruct((nj, t), jnp.float32)],
        grid=(t // tm,),
        in_specs=[pl.BlockSpec((tm, d), lambda i: (i, 0)), full(g), full(w_q), full(keys)],
        out_specs=[pl.BlockSpec((tm, d), lambda i: (i, 0)),
                   pl.BlockSpec((nj, tm), lambda i: (0, i)),
                   pl.BlockSpec((nj, tm), lambda i: (0, i))],
        scratch_shapes=[small(), small(), small(), small(), big(), big(), small(), small()],
        compiler_params=_cparams("parallel"),
        name="peer_route",
    )(h, g, w_q, keys)


def _gelu(x):
    return 0.5 * x * (1.0 + lax.erf(x * (2.0 ** -0.5)))


def _transpose_tiles(x):
    sub = lax.broadcasted_iota(jnp.int32, (1, SUBLANES, LANES), 1)
    parts = [x[:, r] for r in range(SUBLANES)]
    for s in (4, 2, 1):
        low = (sub & s) == 0
        new = []
        for r in range(SUBLANES):
            other = parts[r ^ s]
            if r & s == 0:
                new.append(jnp.where(low, parts[r], pltpu.roll(other, s, axis=1)))
            else:
                new.append(jnp.where(low, pltpu.roll(other, SUBLANES - s, axis=1), parts[r]))
        parts = new
    return jnp.stack(parts, axis=1)


def _experts_kernel(idx_ref, idxn_ref, gate_ref, xn_ref, h_ref, uv_hbm, o_ref, buf, tbuf, sem):
    f32, bf = jnp.float32, jnp.bfloat16
    tt = xn_ref.shape[0] // 2
    d = xn_ref.shape[1]
    nj = gate_ref.shape[1]
    nrows = tt * nj
    nchunk = d // LANES
    ngroups = nrows // SUBLANES
    i = pl.program_id(0)
    n = pl.num_programs(0)

    def start_group(iref, base, sl, g):
        for k in range(SUBLANES):
            r = g * SUBLANES + k
            e = iref[0, 0, base + r]
            pltpu.make_async_copy(uv_hbm.at[e], buf.at[sl, r], sem.at[sl]).start(priority=k % 2)

    def wait_slot(sl):
        pltpu.make_async_copy(buf.at[sl], buf.at[sl], sem.at[sl]).wait()

    @pl.when(i == 0)
    def _():
        def body(g, carry):
            start_group(idx_ref, 0, 0, g)
            return carry
        lax.fori_loop(0, ngroups, body, 0)

    n_slabs = 4
    gs = ngroups // n_slabs
    nphase = n_slabs + 2 * nchunk
    issue_phases = (nphase * 3) // 4
    per_phase = -(-ngroups // issue_phases)
    col = lax.broadcasted_iota(jnp.int32, (tt, nrows), 1)
    row = lax.broadcasted_iota(jnp.int32, (tt, nrows), 0)
    own = (col >= row * nj) & (col < (row + 1) * nj)

    def block(sl, next_iref, next_base):
        tok = slice(sl * tt, (sl + 1) * tt)

        def start_phase(ph):
            for g in range(ph * per_phase, min((ph + 1) * per_phase, ngroups)):
                start_group(next_iref, next_base, 1 - sl, g)

        def words(c):
            return tbuf[:, c, :, :].reshape(nrows, LANES)

        wait_slot(sl)
        for j in range(n_slabs):
            tiles = buf[sl, j * gs * SUBLANES:(j + 1) * gs * SUBLANES].reshape(gs, SUBLANES, nchunk, LANES)
            tbuf[j * gs:(j + 1) * gs] = _transpose_tiles(tiles)
            start_phase(j)
        xn = xn_ref[tok, :].astype(bf)
        act = jnp.zeros((tt, nrows), f32)
        for c in range(nchunk):
            uc = lax.bitcast_convert_type(words(c) & jnp.uint32(0xFFFF0000), f32).astype(bf)
            act += lax.dot_general(xn[:, c * LANES:(c + 1) * LANES], uc, (((1,), (1,)), ((), ())),
                                   preferred_element_type=f32)
            start_phase(n_slabs + c)
        gate = jnp.concatenate([gate_ref[tok, :]] * tt, axis=1)
        w = jnp.where(own, gate * _gelu(act), 0.0).astype(bf)
        for c in range(nchunk):
            vc = lax.bitcast_convert_type(words(c) << 16, f32).astype(bf)
            cs = slice(c * LANES, (c + 1) * LANES)
            o_ref[tok, cs] = h_ref[tok, cs] + jnp.dot(w, vc, preferred_element_type=f32)
            start_phase(n_slabs + nchunk + c)

    block(0, idx_ref, nrows)
    block(1, idxn_ref, 0)

    @pl.when(i == n - 1)
    def _():
        wait_slot(0)


def _experts(idx, gate, xn, h, uv, *, tt):
    t, d = h.shape
    nj = gate.shape[1]
    n = t // (2 * tt)
    nrows = tt * nj
    nchunk = d // LANES
    assert nchunk == SUBLANES
    idx_step = idx.reshape(n, 1, 2 * nrows)
    idx_block = idx.reshape(2 * n, 1, nrows)
    return pl.pallas_call(
        _experts_kernel,
        out_shape=jax.ShapeDtypeStruct((t, d), jnp.float32),
        grid=(n,),
        in_specs=[pl.BlockSpec((1, 1, 2 * nrows), lambda i: (i, 0, 0), memory_space=pltpu.SMEM),
                  pl.BlockSpec((1, 1, nrows), lambda i: (jnp.minimum(2 * i + 2, 2 * n - 2), 0, 0),
                               memory_space=pltpu.SMEM),
                  pl.BlockSpec((2 * tt, nj), lambda i: (i, 0)),
                  pl.BlockSpec((2 * tt, d), lambda i: (i, 0)),
                  pl.BlockSpec((2 * tt, d), lambda i: (i, 0)),
                  pl.BlockSpec(memory_space=pl.ANY)],
        out_specs=pl.BlockSpec((2 * tt, d), lambda i: (i, 0)),
        scratch_shapes=[pltpu.VMEM((2, nrows, nchunk, LANES), jnp.uint32),
                        pltpu.VMEM((nrows // SUBLANES, nchunk, SUBLANES, LANES), jnp.uint32),
                        pltpu.SemaphoreType.DMA((2,))],
        compiler_params=_cparams("arbitrary"),
        name="peer_experts",
    )(idx_step, idx_block, gate, xn, h, uv)


def _final_norm_kernel(x_ref, g_ref, o_ref):
    o_ref[...] = _rms(x_ref[...], g_ref[...])


def _final_norm(h, g, *, tm):
    t, d = h.shape
    tm = min(tm, t)
    return pl.pallas_call(
        _final_norm_kernel,
        out_shape=jax.ShapeDtypeStruct((t, d), jnp.float32),
        grid=(t // tm,),
        in_specs=[pl.BlockSpec((tm, d), lambda i: (i, 0)), pl.BlockSpec((1, d), lambda i: (0, 0))],
        out_specs=pl.BlockSpec((tm, d), lambda i: (i, 0)),
        compiler_params=_cparams("parallel"),
        name="final_norm",
    )(h, g.reshape(1, d))


def _rope_lane_tables(positions, rot_dim, period, lane0):
    half = rot_dim // 2
    inv = 1.0 / (ROPE_THETA ** (jnp.arange(0, rot_dim, 2, dtype=jnp.float32) / rot_dim))
    ang = positions.astype(jnp.float32).reshape(-1, 1) * inv[None, :]
    cos, sin = jnp.cos(ang), jnp.sin(ang)
    lane = np.arange(LANES) % period - lane0
    in_rot = (lane >= 0) & (lane < rot_dim)
    sel = np.where(in_rot, lane % half, 0)
    sign = np.where(lane < half, -1.0, 1.0).astype(np.float32)
    c = jnp.where(in_rot[None, :], cos[:, sel], 1.0)
    s = jnp.where(in_rot[None, :], sin[:, sel] * sign[None, :], 0.0)
    return c, s


def _layer_weights(l, w_in, mla_w_uq, mla_w_ukv):
    bf = jnp.bfloat16
    d = w_in.shape[1]
    offs = np.cumsum([0, CONV_CH, CONV_CH, CONV_CH, 512, 512, 512, MLA_Q_RANK, MLA_KV_RANK, MLA_ROPE])
    w = w_in[l]
    seg = lambda i: w[:, int(offs[i]):int(offs[i + 1])]
    gates = w[:, int(offs[9]):]
    kr = jnp.pad(seg(8), ((0, 0), (0, LANES - MLA_ROPE)))
    w_in_p = jnp.concatenate([gates, seg(0), seg(1), seg(2), seg(3), seg(4), seg(5), seg(6), seg(7), kr],
                             axis=1).astype(bf)
    qd = MLA_NOPE + MLA_ROPE
    wuq = mla_w_uq[l].reshape(MLA_Q_RANK, MLA_HEADS, qd)
    wuq = jnp.pad(wuq, ((0, 0), (0, 0), (0, HEAD_W - qd))).reshape(MLA_Q_RANK, MLA_HEADS * HEAD_W).astype(bf)
    wukv = mla_w_ukv[l].reshape(MLA_KV_RANK, MLA_HEADS, MLA_NOPE + MLA_V)
    wk = jnp.pad(wukv[:, :, :MLA_NOPE], ((0, 0), (0, 0), (0, HEAD_W - MLA_NOPE)))
    wk = wk.reshape(MLA_KV_RANK, MLA_HEADS * HEAD_W).astype(bf)
    wv = wukv[:, :, MLA_NOPE:].reshape(MLA_KV_RANK, MLA_HEADS * MLA_V).astype(bf)
    return w_in_p, wuq, wk, wv


def _pack_tables(u, v):
    hi = lax.bitcast_convert_type(u.astype(jnp.bfloat16), jnp.uint16).astype(jnp.uint32)
    lo = lax.bitcast_convert_type(v.astype(jnp.bfloat16), jnp.uint16).astype(jnp.uint32)
    return ((hi << 16) | lo).reshape(u.shape[0], u.shape[1] // LANES, LANES)


def _placement():
    e = np.zeros((LANES, MLA_HEADS * HEAD_W), np.float32)
    for h in range(MLA_HEADS):
        for c in range(MLA_ROPE):
            e[c, h * HEAD_W + MLA_NOPE + c] = 1.0
    return jnp.asarray(e, jnp.bfloat16)


def kernel(x, mem, positions, norm_mix, w_in, conv_w, diff_lambda, diff_subln, mla_q_norm, mla_w_uq,
           mla_kv_norm, mla_w_ukv, w_branch, w_out, norm_cross, norm_mem, w_cq, w_ckv, w_co, norm_ffn,
           peer_w_q, peer_keys, peer_u, peer_v, final_norm):
    bf = jnp.bfloat16
    batch, seq, d = x.shape
    mem_tokens = mem.shape[1]
    depth = w_in.shape[0]
    t = batch * seq
    h = x.reshape(t, d)
    mem2 = mem.reshape(batch * mem_tokens, d)

    cd, sd = _rope_lane_tables(positions, DIFF_ROT, DIFF_HEAD_DIM, 0)
    cm, sm = _rope_lane_tables(positions, MLA_ROPE, LANES, MLA_NOPE)
    place = _placement()
    tt = 16

    for l in range(depth):
        lam_init = 0.8 - 0.6 * math.exp(-0.3 * l)
        w_in_p, wuq, wk, wv = _layer_weights(l, w_in, mla_w_uq, mla_w_ukv)

        proj = _norm_matmul(h, norm_mix[l], w_in_p, tm=512, tn=PROJ_W // 2, name="in_proj")
        dq, dk, mq, mk, mv = _prep(proj, cd, sd, cm, sm, mla_q_norm[l].reshape(1, -1), wuq,
                                   mla_kv_norm[l].reshape(1, -1), wk, wv, place, tm=512)
        y_b = _flash(dq, dk, proj, OFF_DV, batch, seq, DIFF_HEADS, n_maps=2, tq=ATTN_TQ, tk=ATTN_TK,
                     lam_init=lam_init, lamv=diff_lambda[l], subln=diff_subln[l].reshape(1, -1),
                     name="diff_attn")
        y_c = _flash(mq, mk, mv, 0, batch, seq, MLA_HEADS, n_maps=1, tq=ATTN_TQ, tk=ATTN_TK, name="mla_attn")
        h = _merge(proj, y_b, y_c, h, conv_w[l], w_branch[l].astype(bf), w_out[l].astype(bf), seq, tm=256)

        memkv = _norm_matmul(mem2, norm_mem[l], w_ckv[l].astype(bf), tm=256, tn=512, name="mem_kv")
        h = _cross(h, norm_cross[l].reshape(1, d), w_cq[l].astype(bf), memkv, w_co[l].astype(bf),
                   seq, mem_tokens, tm=256)

        xn, idx, gate = _route(h, norm_ffn[l].reshape(1, d), peer_w_q[l].astype(bf),
                               peer_keys[l].astype(bf), tm=256)
        uv = _pack_tables(peer_u[l], peer_v[l])
        h = _experts(idx.T, gate.T, xn, h, uv, tt=tt)

    out = _final_norm(h, final_norm, tm=512)
    return out.reshape(batch, seq, d)
```

```python
import functools
import math

import numpy as np
import jax
import jax.numpy as jnp
from jax import lax
from jax.experimental import pallas as pl
from jax.experimental.pallas import tpu as pltpu

EPS = 1e-6
ROPE_THETA = 500000.0
CONV_CH = 512
DIFF_HEADS = 4
DIFF_HEAD_DIM = 64
DIFF_ROT = DIFF_HEAD_DIM // 4
MLA_HEADS = 4
MLA_Q_RANK = 256
MLA_KV_RANK = 128
MLA_NOPE = 64
MLA_ROPE = 32
MLA_V = 128
N_BRANCH = 3
CROSS_HEADS = 4
PEER_HEADS = 8
PEER_N_KEYS = 128
PEER_KEY_DIM = 128
PEER_TOPK = 16

LANES = 128
SUBLANES = 8
VMEM_LIMIT_BYTES = 56 * 1024 * 1024

ATTN_TQ = 512
ATTN_TK = 8192

HEAD_W = 128
OFF_GATES = 0
OFF_AB = 3072
OFF_AC = OFF_AB + CONV_CH
OFF_AH = OFF_AC + CONV_CH
OFF_DQ = OFF_AH + CONV_CH
OFF_DK = OFF_DQ + 512
OFF_DV = OFF_DK + 512
OFF_CQ = OFF_DV + 512
OFF_CKV = OFF_CQ + MLA_Q_RANK
OFF_KR = OFF_CKV + MLA_KV_RANK
PROJ_W = OFF_KR + LANES


def _cparams(*sem):
    return pltpu.CompilerParams(dimension_semantics=sem, vmem_limit_bytes=VMEM_LIMIT_BYTES)


def _norm_matmul_kernel(x_ref, g_ref, w_ref, o_ref, xn_ref):
    @pl.when(pl.program_id(1) == 0)
    def _():
        x = x_ref[...]
        ms = jnp.mean(x * x, axis=-1, keepdims=True)
        xn_ref[...] = (x * lax.rsqrt(ms + EPS) * g_ref[...]).astype(xn_ref.dtype)

    o_ref[...] = jnp.dot(xn_ref[...], w_ref[...],
                         preferred_element_type=jnp.float32).astype(o_ref.dtype)


def _norm_matmul(x, g, w, *, tm, tn, name):
    t, k = x.shape
    n = w.shape[1]
    tm = min(tm, t)
    return pl.pallas_call(
        _norm_matmul_kernel,
        out_shape=jax.ShapeDtypeStruct((t, n), jnp.bfloat16),
        grid=(t // tm, n // tn),
        in_specs=[pl.BlockSpec((tm, k), lambda i, j: (i, 0)),
                  pl.BlockSpec((1, k), lambda i, j: (0, 0)),
                  pl.BlockSpec((k, tn), lambda i, j: (0, j))],
        out_specs=pl.BlockSpec((tm, tn), lambda i, j: (i, j)),
        scratch_shapes=[pltpu.VMEM((tm, k), jnp.bfloat16)],
        compiler_params=_cparams("parallel", "arbitrary"),
        name=name,
    )(x, g.reshape(1, k), w)


def _rope_block(xb, c, s, lo_mask, shift):
    partner = jnp.where(lo_mask, pltpu.roll(xb, LANES - shift, axis=1), pltpu.roll(xb, shift, axis=1))
    return xb * c + partner * s


def _rms(x, g):
    ms = jnp.mean(x * x, axis=-1, keepdims=True)
    return x * lax.rsqrt(ms + EPS) * g


def _prep_kernel(dq_ref, dk_ref, cq_ref, ckv_ref, kr_ref, cd_ref, sd_ref, cm_ref, sm_ref,
                 qn_ref, wuq_ref, kvn_ref, wk_ref, wv_ref, place_ref,
                 dq_o, dk_o, mq_o, mk_o, mv_o):
    f32 = jnp.float32
    tm = dq_ref.shape[0]
    lane = lax.broadcasted_iota(jnp.int32, (tm, LANES), 1)
    cd, sd, cm, sm = cd_ref[...], sd_ref[...], cm_ref[...], sm_ref[...]
    d_lo = (lane % DIFF_HEAD_DIM) < (DIFF_ROT // 2)
    m_lo = lane < (MLA_NOPE + MLA_ROPE // 2)
    d_scale = DIFF_HEAD_DIM ** -0.5
    m_scale = (MLA_NOPE + MLA_ROPE) ** -0.5
    first_half = lane < DIFF_HEAD_DIM

    for h in range(DIFF_HEADS):
        sl = slice(h * HEAD_W, (h + 1) * HEAD_W)
        qb = _rope_block(dq_ref[:, sl].astype(f32), cd, sd, d_lo, DIFF_ROT // 2) * d_scale
        kb = _rope_block(dk_ref[:, sl].astype(f32), cd, sd, d_lo, DIFF_ROT // 2)
        dq_o[:, 2 * h * HEAD_W:(2 * h + 1) * HEAD_W] = jnp.where(first_half, qb, 0.0).astype(dq_o.dtype)
        dq_o[:, (2 * h + 1) * HEAD_W:(2 * h + 2) * HEAD_W] = jnp.where(first_half, 0.0, qb).astype(dq_o.dtype)
        dk_o[:, sl] = kb.astype(dk_o.dtype)

    cqn = _rms(cq_ref[...].astype(f32), qn_ref[...]).astype(jnp.bfloat16)
    q = jnp.dot(cqn, wuq_ref[...], preferred_element_type=f32)
    ckvn = _rms(ckv_ref[...].astype(f32), kvn_ref[...]).astype(jnp.bfloat16)
    k = jnp.dot(ckvn, wk_ref[...], preferred_element_type=f32)
    k = k + jnp.dot(kr_ref[...], place_ref[...], preferred_element_type=f32)
    v = jnp.dot(ckvn, wv_ref[...], preferred_element_type=f32)
    for h in range(MLA_HEADS):
        sl = slice(h * HEAD_W, (h + 1) * HEAD_W)
        mq_o[:, sl] = (_rope_block(q[:, sl], cm, sm, m_lo, MLA_ROPE // 2) * m_scale).astype(mq_o.dtype)
        mk_o[:, sl] = _rope_block(k[:, sl], cm, sm, m_lo, MLA_ROPE // 2).astype(mk_o.dtype)
    mv_o[...] = v.astype(mv_o.dtype)


def _prep(proj, cd, sd, cm, sm, qn, wuq, kvn, wk, wv, place, *, tm):
    t = proj.shape[0]
    tm = min(tm, t)
    row = lambda w, off: pl.BlockSpec((tm, w), lambda i, _o=off // w: (i, _o))
    full = lambda a: pl.BlockSpec(a.shape, lambda i: (0,) * a.ndim)
    tab = pl.BlockSpec((tm, LANES), lambda i: (i, 0))
    bf = jnp.bfloat16
    return pl.pallas_call(
        _prep_kernel,
        out_shape=[jax.ShapeDtypeStruct((t, 2 * DIFF_HEADS * HEAD_W), bf),
                   jax.ShapeDtypeStruct((t, DIFF_HEADS * HEAD_W), bf),
                   jax.ShapeDtypeStruct((t, MLA_HEADS * HEAD_W), bf),
                   jax.ShapeDtypeStruct((t, MLA_HEADS * HEAD_W), bf),
                   jax.ShapeDtypeStruct((t, MLA_HEADS * HEAD_W), bf)],
        grid=(t // tm,),
        in_specs=[row(512, OFF_DQ), row(512, OFF_DK), row(MLA_Q_RANK, OFF_CQ),
                  row(MLA_KV_RANK, OFF_CKV), row(LANES, OFF_KR), tab, tab, tab, tab,
                  full(qn), full(wuq), full(kvn), full(wk), full(wv), full(place)],
        out_specs=[pl.BlockSpec((tm, 2 * DIFF_HEADS * HEAD_W), lambda i: (i, 0)),
                   pl.BlockSpec((tm, DIFF_HEADS * HEAD_W), lambda i: (i, 0)),
                   pl.BlockSpec((tm, MLA_HEADS * HEAD_W), lambda i: (i, 0)),
                   pl.BlockSpec((tm, MLA_HEADS * HEAD_W), lambda i: (i, 0)),
                   pl.BlockSpec((tm, MLA_HEADS * HEAD_W), lambda i: (i, 0))],
        compiler_params=_cparams("parallel"),
        name="attn_prep",
    )(proj, proj, proj, proj, proj, cd, sd, cm, sm, qn, wuq, kvn, wk, wv, place)


def _flash_kernel(*refs, n_maps, tk, lam_init):
    if n_maps == 2:
        q_ref, k_ref, v_ref, lamv_ref, subln_ref, o_ref, m_ref, l_ref, acc_ref = refs
    else:
        q_ref, k_ref, v_ref, o_ref, m_ref, l_ref, acc_ref = refs
    f32 = jnp.float32
    tq = q_ref.shape[0]
    nk = k_ref.shape[0] // tk
    m_ref[...] = jnp.full(m_ref.shape, -jnp.inf, f32)
    l_ref[...] = jnp.zeros(l_ref.shape, f32)
    acc_ref[...] = jnp.zeros(acc_ref.shape, f32)

    def body(c, carry):
        start = pl.multiple_of(c * tk, tk)
        kc = k_ref[pl.ds(start, tk), :]
        vc = v_ref[pl.ds(start, tk), :]
        for mi in range(n_maps):
            q = q_ref[:, mi * HEAD_W:(mi + 1) * HEAD_W]
            s = lax.dot_general(q, kc, (((1,), (1,)), ((), ())), preferred_element_type=f32)
            m_prev = m_ref[mi]
            m_new = jnp.maximum(m_prev, jnp.max(s, axis=-1, keepdims=True))
            alpha = jnp.exp(m_prev - m_new)
            p = jnp.exp(s - m_new[:, :1])
            l_ref[mi] = alpha * l_ref[mi] + jnp.sum(p, axis=-1, keepdims=True)
            acc_ref[mi] = alpha * acc_ref[mi] + jnp.dot(p.astype(vc.dtype), vc, preferred_element_type=f32)
            m_ref[mi] = m_new
        return carry

    lax.fori_loop(0, nk, body, 0)

    if n_maps == 2:
        lv = lamv_ref[...]
        lam = (jnp.exp(jnp.sum(lv[0:1] * lv[1:2], axis=-1, keepdims=True))
               - jnp.exp(jnp.sum(lv[2:3] * lv[3:4], axis=-1, keepdims=True)) + lam_init)
        o = acc_ref[0] / l_ref[0] - lam * (acc_ref[1] / l_ref[1])
        o = _rms(o, subln_ref[...]) * (1.0 - lam_init)
    else:
        o = acc_ref[0] / l_ref[0]
    o_ref[...] = o.astype(o_ref.dtype)


def _flash(q, k, v, v_col0, batch, seq, heads, *, n_maps, tq, tk, lam_init=0.0, lamv=None, subln=None, name):
    tq = min(tq, seq)
    tk = min(tk, seq)
    nq = seq // tq
    vb = v_col0 // HEAD_W
    in_specs = [pl.BlockSpec((tq, n_maps * HEAD_W), lambda b, h, i: (b * nq + i, h)),
                pl.BlockSpec((seq, HEAD_W), lambda b, h, i: (b, h)),
                pl.BlockSpec((seq, HEAD_W), lambda b, h, i: (b, vb + h))]
    args = [q, k, v]
    if n_maps == 2:
        in_specs += [pl.BlockSpec(lamv.shape, lambda b, h, i: (0, 0)),
                     pl.BlockSpec(subln.shape, lambda b, h, i: (0, 0))]
        args += [lamv, subln]
    return pl.pallas_call(
        functools.partial(_flash_kernel, n_maps=n_maps, tk=tk, lam_init=lam_init),
        out_shape=jax.ShapeDtypeStruct((batch * seq, heads * HEAD_W), jnp.bfloat16),
        grid=(batch, heads, nq),
        in_specs=in_specs,
        out_specs=pl.BlockSpec((tq, HEAD_W), lambda b, h, i: (b * nq + i, h)),
        scratch_shapes=[pltpu.VMEM((n_maps, tq, HEAD_W), jnp.float32),
                        pltpu.VMEM((n_maps, tq, HEAD_W), jnp.float32),
                        pltpu.VMEM((n_maps, tq, HEAD_W), jnp.float32)],
        compiler_params=_cparams("parallel", "parallel", "arbitrary"),
        name=name,
    )(*args)


def _sigmoid(x):
    return 1.0 / (1.0 + jnp.exp(-x))


def _merge_kernel(g0_ref, g1_ref, g2_ref, ab_ref, ac_ref, ah_ref, pc_ref, ph_ref, nc_ref, nh_ref,
                  yb_ref, yc_ref, h_ref, cw_ref, wb_ref, wo_ref, o_ref, *, tiles_per_seq):
    f32 = jnp.float32
    tm = ab_ref.shape[0]
    i = pl.program_id(0)
    pos = i % tiles_per_seq
    z = ac_ref[...].astype(f32) * ah_ref[...].astype(f32)
    zp_edge = pc_ref[SUBLANES - 1:SUBLANES, :].astype(f32) * ph_ref[SUBLANES - 1:SUBLANES, :].astype(f32)
    zn_edge = nc_ref[0:1, :].astype(f32) * nh_ref[0:1, :].astype(f32)
    zp_edge = jnp.where(pos > 0, zp_edge, 0.0)
    zn_edge = jnp.where(pos < tiles_per_seq - 1, zn_edge, 0.0)
    row = lax.broadcasted_iota(jnp.int32, z.shape, 0)
    zp = jnp.where(row == 0, zp_edge, pltpu.roll(z, 1, axis=0))
    zn = jnp.where(row == tm - 1, zn_edge, pltpu.roll(z, tm - 1, axis=0))
    cw = cw_ref[...]
    y_a = ab_ref[...].astype(f32) * (cw[0:1] * zp + cw[1:2] * z + cw[2:3] * zn)

    bf = jnp.bfloat16
    merged = _sigmoid(g0_ref[...].astype(f32)) * jnp.dot(y_a.astype(bf), wb_ref[0], preferred_element_type=f32)
    merged += _sigmoid(g1_ref[...].astype(f32)) * jnp.dot(yb_ref[...], wb_ref[1], preferred_element_type=f32)
    merged += _sigmoid(g2_ref[...].astype(f32)) * jnp.dot(yc_ref[...], wb_ref[2], preferred_element_type=f32)
    o_ref[...] = h_ref[...] + jnp.dot(merged.astype(bf), wo_ref[...], preferred_element_type=f32)


def _merge(proj, y_b, y_c, h, conv_w, w_branch, w_out, seq, *, tm):
    t, d = h.shape
    tm = min(tm, seq)
    tiles_per_seq = seq // tm
    r8 = tm // SUBLANES
    nrow8 = t // SUBLANES
    row = lambda w, off: pl.BlockSpec((tm, w), lambda i, _o=off // w: (i, _o))
    prev = lambda off: pl.BlockSpec((SUBLANES, CONV_CH),
                                    lambda i, _o=off // CONV_CH: (jnp.maximum(i * r8 - 1, 0), _o))
    nxt = lambda off: pl.BlockSpec((SUBLANES, CONV_CH),
                                   lambda i, _o=off // CONV_CH: (jnp.minimum((i + 1) * r8, nrow8 - 1), _o))
    full = lambda a: pl.BlockSpec(a.shape, lambda i: (0,) * a.ndim)
    return pl.pallas_call(
        functools.partial(_merge_kernel, tiles_per_seq=tiles_per_seq),
        out_shape=jax.ShapeDtypeStruct((t, d), jnp.float32),
        grid=(t // tm,),
        in_specs=[row(d, OFF_GATES), row(d, OFF_GATES + d), row(d, OFF_GATES + 2 * d),
                  row(CONV_CH, OFF_AB), row(CONV_CH, OFF_AC), row(CONV_CH, OFF_AH),
                  prev(OFF_AC), prev(OFF_AH), nxt(OFF_AC), nxt(OFF_AH),
                  pl.BlockSpec((tm, y_b.shape[1]), lambda i: (i, 0)),
                  pl.BlockSpec((tm, y_c.shape[1]), lambda i: (i, 0)),
                  pl.BlockSpec((tm, d), lambda i: (i, 0)),
                  full(conv_w), full(w_branch), full(w_out)],
        out_specs=pl.BlockSpec((tm, d), lambda i: (i, 0)),
        compiler_params=_cparams("parallel"),
        name="gated_merge",
    )(proj, proj, proj, proj, proj, proj, proj, proj, proj, proj, y_b, y_c, h, conv_w, w_branch, w_out)


def _cross_kernel(h_ref, g_ref, wq_ref, kv_ref, wo_ref, o_ref):
    f32, bf = jnp.float32, jnp.bfloat16
    h = h_ref[...]
    d = h.shape[1]
    hd = d // CROSS_HEADS
    hn = _rms(h, g_ref[...]).astype(bf)
    q = (jnp.dot(hn, wq_ref[...], preferred_element_type=f32) * (hd ** -0.5)).astype(bf)
    outs = []
    for hh in range(CROSS_HEADS):
        kh = kv_ref[:, hh * hd:(hh + 1) * hd]
        vh = kv_ref[:, d + hh * hd:d + (hh + 1) * hd]
        s = lax.dot_general(q[:, hh * hd:(hh + 1) * hd], kh, (((1,), (1,)), ((), ())),
                            preferred_element_type=f32)
        p = jnp.exp(s - jnp.max(s, axis=-1, keepdims=True))
        p = p / jnp.sum(p, axis=-1, keepdims=True)
        outs.append(jnp.dot(p.astype(bf), vh, preferred_element_type=f32).astype(bf))
    o = jnp.concatenate(outs, axis=-1)
    o_ref[...] = h + jnp.dot(o, wo_ref[...], preferred_element_type=f32)


def _cross(h, g, w_cq, memkv, w_co, seq, mem_tokens, *, tm):
    t, d = h.shape
    tm = min(tm, seq)
    tiles_per_seq = seq // tm
    full = lambda a: pl.BlockSpec(a.shape, lambda i: (0,) * a.ndim)
    return pl.pallas_call(
        _cross_kernel,
        out_shape=jax.ShapeDtypeStruct((t, d), jnp.float32),
        grid=(t // tm,),
        in_specs=[pl.BlockSpec((tm, d), lambda i: (i, 0)), full(g), full(w_cq),
                  pl.BlockSpec((mem_tokens, 2 * d), lambda i: (i // tiles_per_seq, 0)), full(w_co)],
        out_specs=pl.BlockSpec((tm, d), lambda i: (i, 0)),
        compiler_params=_cparams("parallel"),
        name="cross_attn",
    )(h, g, w_cq, memkv, w_co)


def _topk_rows(sc, k, vals_ref, idx_ref, payload=None):
    n = sc.shape[0]
    row = lax.broadcasted_iota(jnp.int32, sc.shape, 0).astype(jnp.float32)
    for r in range(k):
        m = jnp.max(sc, axis=0, keepdims=True)
        first = jnp.min(jnp.where(sc == m, row, float(n)), axis=0, keepdims=True)
        hit = row == first
        vals_ref[r:r + 1, :] = m
        if payload is None:
            idx_ref[r:r + 1, :] = first
        else:
            idx_ref[r:r + 1, :] = jnp.sum(jnp.where(hit, payload, 0.0), axis=0, keepdims=True)
        sc = jnp.where(hit, -jnp.inf, sc)


def _route_kernel(h_ref, g_ref, wq_ref, keys_ref, xn_o, idx_o, gate_o,
                  s1_ref, i1_ref, s2_ref, i2_ref, cs_ref, ci_ref, ts_ref, ti_ref):
    f32, bf = jnp.float32, jnp.bfloat16
    xn = _rms(h_ref[...], g_ref[...])
    xn_o[...] = xn
    q = jnp.dot(xn.astype(bf), wq_ref[...], preferred_element_type=f32).astype(bf)
    kd = PEER_KEY_DIM
    for hh in range(PEER_HEADS):
        for p, (s_ref, i_ref) in enumerate(((s1_ref, i1_ref), (s2_ref, i2_ref))):
            c0 = (hh * 2 + p) * kd
            sc = lax.dot_general(keys_ref[hh, p], q[:, c0:c0 + kd], (((1,), (1,)), ((), ())),
                                 preferred_element_type=f32)
            _topk_rows(sc, PEER_TOPK, s_ref, i_ref)
        nk = float(PEER_N_KEYS)
        cs_ref[0:PEER_TOPK, :] = s1_ref[0:1, :] + s2_ref[...]
        ci_ref[0:PEER_TOPK, :] = i1_ref[0:1, :] * nk + i2_ref[...]
        off = PEER_TOPK
        for a in range(1, PEER_TOPK // 2):
            cs_ref[off:off + SUBLANES, :] = s1_ref[a:a + 1, :] + s2_ref[0:SUBLANES, :]
            ci_ref[off:off + SUBLANES, :] = i1_ref[a:a + 1, :] * nk + i2_ref[0:SUBLANES, :]
            off += SUBLANES
        cs_ref[off:off + SUBLANES, :] = s1_ref[PEER_TOPK // 2:PEER_TOPK, :] + s2_ref[0:1, :]
        ci_ref[off:off + SUBLANES, :] = i1_ref[PEER_TOPK // 2:PEER_TOPK, :] * nk + i2_ref[0:1, :]
        _topk_rows(cs_ref[...], PEER_TOPK, ts_ref, ti_ref, payload=ci_ref[...])
        ts = ts_ref[...]
        e = jnp.exp(ts - ts[0:1, :])
        gate_o[hh * PEER_TOPK:(hh + 1) * PEER_TOPK, :] = e / jnp.sum(e, axis=0, keepdims=True)
        idx_o[hh * PEER_TOPK:(hh + 1) * PEER_TOPK, :] = ti_ref[...].astype(jnp.int32)


def _route(h, g, w_q, keys, *, tm):
    t, d = h.shape
    tm = min(tm, t)
    nj = PEER_HEADS * PEER_TOPK
    full = lambda a: pl.BlockSpec(a.shape, lambda i: (0,) * a.ndim)
    assert PEER_TOPK == 2 * SUBLANES
    n_cand = PEER_TOPK + (PEER_TOPK // 2) * SUBLANES
    small = lambda: pltpu.VMEM((PEER_TOPK, tm), jnp.float32)
    big = lambda: pltpu.VMEM((n_cand, tm), jnp.float32)
    return pl.pallas_call(
        _route_kernel,
        out_shape=[jax.ShapeDtypeStruct((t, d), jnp.float32),
                   jax.ShapeDtypeStruct((nj, t), jnp.int32),
                   jax.ShapeDtypeStruct((nj, t), jnp.float32)],
        grid=(t // tm,),
        in_specs=[pl.BlockSpec((tm, d), lambda i: (i, 0)), full(g), full(w_q), full(keys)],
        out_specs=[pl.BlockSpec((tm, d), lambda i: (i, 0)),
                   pl.BlockSpec((nj, tm), lambda i: (0, i)),
                   pl.BlockSpec((nj, tm), lambda i: (0, i))],
        scratch_shapes=[small(), small(), small(), small(), big(), big(), small(), small()],
        compiler_params=_cparams("parallel"),
        name="peer_route",
    )(h, g, w_q, keys)


def _gelu(x):
    return 0.5 * x * (1.0 + lax.erf(x * (2.0 ** -0.5)))


def _transpose_tiles(x):
    sub = lax.broadcasted_iota(jnp.int32, (1, SUBLANES, LANES), 1)
    parts = [x[:, r] for r in range(SUBLANES)]
    for s in (4, 2, 1):
        low = (sub & s) == 0
        new = []
        for r in range(SUBLANES):
            other = parts[r ^ s]
            if r & s == 0:
                new.append(jnp.where(low, parts[r], pltpu.roll(other, s, axis=1)))
            else:
                new.append(jnp.where(low, pltpu.roll(other, SUBLANES - s, axis=1), parts[r]))
        parts = new
    return jnp.stack(parts, axis=1)


def _experts_kernel(idx_ref, idxn_ref, gate_ref, xn_ref, h_ref, uv_hbm, o_ref, buf, tbuf, sem):
    f32, bf = jnp.float32, jnp.bfloat16
    tt = xn_ref.shape[0] // 2
    d = xn_ref.shape[1]
    nj = gate_ref.shape[1]
    nrows = tt * nj
    nchunk = d // LANES
    ngroups = nrows // SUBLANES
    i = pl.program_id(0)
    n = pl.num_programs(0)

    def start_group(iref, base, sl, g):
        for k in range(SUBLANES):
            r = g * SUBLANES + k
            e = iref[0, 0, base + r]
            pltpu.make_async_copy(uv_hbm.at[e], buf.at[sl, r], sem.at[sl]).start(priority=k % 2)

    def wait_slot(sl):
        pltpu.make_async_copy(buf.at[sl], buf.at[sl], sem.at[sl]).wait()

    @pl.when(i == 0)
    def _():
        def body(g, carry):
            start_group(idx_ref, 0, 0, g)
            return carry
        lax.fori_loop(0, ngroups, body, 0)

    slab_groups = 4
    n_slabs = ngroups // slab_groups
    row_blocks = 4
    rb_groups = ngroups // row_blocks
    rb_rows = rb_groups * SUBLANES
    starts_per_slab = 3
    starts_per_piece = -(-(ngroups - starts_per_slab * n_slabs) // (row_blocks * nchunk))
    col = lax.broadcasted_iota(jnp.int32, (tt, nrows), 1)
    row = lax.broadcasted_iota(jnp.int32, (tt, nrows), 0)
    own = (col >= row * nj) & (col < (row + 1) * nj)

    def block(sl, next_iref, next_base):
        tok = slice(sl * tt, (sl + 1) * tt)
        pending = iter(range(ngroups))

        def start_groups(count):
            for _ in range(count):
                g = next(pending, None)
                if g is not None:
                    start_group(next_iref, next_base, 1 - sl, g)

        def words(rb, c):
            return tbuf[rb * rb_groups:(rb + 1) * rb_groups, c, :, :].reshape(rb_rows, LANES)

        wait_slot(sl)
        for j in range(n_slabs):
            rows = slice(j * slab_groups * SUBLANES, (j + 1) * slab_groups * SUBLANES)
            tiles = buf[sl, rows].reshape(slab_groups, SUBLANES, nchunk, LANES)
            tbuf[j * slab_groups:(j + 1) * slab_groups] = _transpose_tiles(tiles)
            start_groups(starts_per_slab)
        xn = xn_ref[tok, :].astype(bf)
        acts = []
        for rb in range(row_blocks):
            a = jnp.zeros((tt, rb_rows), f32)
            for c in range(nchunk):
                uc = lax.bitcast_convert_type(words(rb, c) & jnp.uint32(0xFFFF0000), f32).astype(bf)
                a += lax.dot_general(xn[:, c * LANES:(c + 1) * LANES], uc, (((1,), (1,)), ((), ())),
                                     preferred_element_type=f32)
                start_groups(starts_per_piece)
            acts.append(a)
        start_groups(ngroups)
        act = jnp.concatenate(acts, axis=1)
        gate = jnp.concatenate([gate_ref[tok, :]] * tt, axis=1)
        w = jnp.where(own, gate * _gelu(act), 0.0).astype(bf)
        for c in range(nchunk):
            acc = None
            for rb in range(row_blocks):
                vc = lax.bitcast_convert_type(words(rb, c) << 16, f32).astype(bf)
                part = jnp.dot(w[:, rb * rb_rows:(rb + 1) * rb_rows], vc, preferred_element_type=f32)
                acc = part if acc is None else acc + part
            cs = slice(c * LANES, (c + 1) * LANES)
            o_ref[tok, cs] = h_ref[tok, cs] + acc

    block(0, idx_ref, nrows)
    block(1, idxn_ref, 0)

    @pl.when(i == n - 1)
    def _():
        wait_slot(0)


def _experts(idx, gate, xn, h, uv, *, tt):
    t, d = h.shape
    nj = gate.shape[1]
    n = t // (2 * tt)
    nrows = tt * nj
    nchunk = d // LANES
    assert nchunk == SUBLANES
    idx_step = idx.reshape(n, 1, 2 * nrows)
    idx_block = idx.reshape(2 * n, 1, nrows)
    return pl.pallas_call(
        _experts_kernel,
        out_shape=jax.ShapeDtypeStruct((t, d), jnp.float32),
        grid=(n,),
        in_specs=[pl.BlockSpec((1, 1, 2 * nrows), lambda i: (i, 0, 0), memory_space=pltpu.SMEM),
                  pl.BlockSpec((1, 1, nrows), lambda i: (jnp.minimum(2 * i + 2, 2 * n - 2), 0, 0),
                               memory_space=pltpu.SMEM),
                  pl.BlockSpec((2 * tt, nj), lambda i: (i, 0)),
                  pl.BlockSpec((2 * tt, d), lambda i: (i, 0)),
                  pl.BlockSpec((2 * tt, d), lambda i: (i, 0)),
                  pl.BlockSpec(memory_space=pl.ANY)],
        out_specs=pl.BlockSpec((2 * tt, d), lambda i: (i, 0)),
        scratch_shapes=[pltpu.VMEM((2, nrows, nchunk, LANES), jnp.uint32),
                        pltpu.VMEM((nrows // SUBLANES, nchunk, SUBLANES, LANES), jnp.uint32),
                        pltpu.SemaphoreType.DMA((2,))],
        compiler_params=_cparams("arbitrary"),
        name="peer_experts",
    )(idx_step, idx_block, gate, xn, h, uv)


def _final_norm_kernel(x_ref, g_ref, o_ref):
    o_ref[...] = _rms(x_ref[...], g_ref[...])


def _final_norm(h, g, *, tm):
    t, d = h.shape
    tm = min(tm, t)
    return pl.pallas_call(
        _final_norm_kernel,
        out_shape=jax.ShapeDtypeStruct((t, d), jnp.float32),
        grid=(t // tm,),
        in_specs=[pl.BlockSpec((tm, d), lambda i: (i, 0)), pl.BlockSpec((1, d), lambda i: (0, 0))],
        out_specs=pl.BlockSpec((tm, d), lambda i: (i, 0)),
        compiler_params=_cparams("parallel"),
        name="final_norm",
    )(h, g.reshape(1, d))


def _rope_lane_tables(positions, rot_dim, period, lane0):
    half = rot_dim // 2
    inv = 1.0 / (ROPE_THETA ** (jnp.arange(0, rot_dim, 2, dtype=jnp.float32) / rot_dim))
    ang = positions.astype(jnp.float32).reshape(-1, 1) * inv[None, :]
    cos, sin = jnp.cos(ang), jnp.sin(ang)
    lane = np.arange(LANES) % period - lane0
    in_rot = (lane >= 0) & (lane < rot_dim)
    sel = np.where(in_rot, lane % half, 0)
    sign = np.where(lane < half, -1.0, 1.0).astype(np.float32)
    c = jnp.where(in_rot[None, :], cos[:, sel], 1.0)
    s = jnp.where(in_rot[None, :], sin[:, sel] * sign[None, :], 0.0)
    return c, s


def _layer_weights(l, w_in, mla_w_uq, mla_w_ukv):
    bf = jnp.bfloat16
    d = w_in.shape[1]
    offs = np.cumsum([0, CONV_CH, CONV_CH, CONV_CH, 512, 512, 512, MLA_Q_RANK, MLA_KV_RANK, MLA_ROPE])
    w = w_in[l]
    seg = lambda i: w[:, int(offs[i]):int(offs[i + 1])]
    gates = w[:, int(offs[9]):]
    kr = jnp.pad(seg(8), ((0, 0), (0, LANES - MLA_ROPE)))
    w_in_p = jnp.concatenate([gates, seg(0), seg(1), seg(2), seg(3), seg(4), seg(5), seg(6), seg(7), kr],
                             axis=1).astype(bf)
    qd = MLA_NOPE + MLA_ROPE
    wuq = mla_w_uq[l].reshape(MLA_Q_RANK, MLA_HEADS, qd)
    wuq = jnp.pad(wuq, ((0, 0), (0, 0), (0, HEAD_W - qd))).reshape(MLA_Q_RANK, MLA_HEADS * HEAD_W).astype(bf)
    wukv = mla_w_ukv[l].reshape(MLA_KV_RANK, MLA_HEADS, MLA_NOPE + MLA_V)
    wk = jnp.pad(wukv[:, :, :MLA_NOPE], ((0, 0), (0, 0), (0, HEAD_W - MLA_NOPE)))
    wk = wk.reshape(MLA_KV_RANK, MLA_HEADS * HEAD_W).astype(bf)
    wv = wukv[:, :, MLA_NOPE:].reshape(MLA_KV_RANK, MLA_HEADS * MLA_V).astype(bf)
    return w_in_p, wuq, wk, wv


def _pack_tables(u, v):
    hi = lax.bitcast_convert_type(u.astype(jnp.bfloat16), jnp.uint16).astype(jnp.uint32)
    lo = lax.bitcast_convert_type(v.astype(jnp.bfloat16), jnp.uint16).astype(jnp.uint32)
    return ((hi << 16) | lo).reshape(u.shape[0], u.shape[1] // LANES, LANES)


def _placement():
    e = np.zeros((LANES, MLA_HEADS * HEAD_W), np.float32)
    for h in range(MLA_HEADS):
        for c in range(MLA_ROPE):
            e[c, h * HEAD_W + MLA_NOPE + c] = 1.0
    return jnp.asarray(e, jnp.bfloat16)


def kernel(x, mem, positions, norm_mix, w_in, conv_w, diff_lambda, diff_subln, mla_q_norm, mla_w_uq,
           mla_kv_norm, mla_w_ukv, w_branch, w_out, norm_cross, norm_mem, w_cq, w_ckv, w_co, norm_ffn,
           peer_w_q, peer_keys, peer_u, peer_v, final_norm):
    bf = jnp.bfloat16
    batch, seq, d = x.shape
    mem_tokens = mem.shape[1]
    depth = w_in.shape[0]
    t = batch * seq
    h = x.reshape(t, d)
    mem2 = mem.reshape(batch * mem_tokens, d)

    cd, sd = _rope_lane_tables(positions, DIFF_ROT, DIFF_HEAD_DIM, 0)
    cm, sm = _rope_lane_tables(positions, MLA_ROPE, LANES, MLA_NOPE)
    place = _placement()
    tt = 16

    for l in range(depth):
        lam_init = 0.8 - 0.6 * math.exp(-0.3 * l)
        w_in_p, wuq, wk, wv = _layer_weights(l, w_in, mla_w_uq, mla_w_ukv)

        proj = _norm_matmul(h, norm_mix[l], w_in_p, tm=512, tn=PROJ_W // 2, name="in_proj")
        dq, dk, mq, mk, mv = _prep(proj, cd, sd, cm, sm, mla_q_norm[l].reshape(1, -1), wuq,
                                   mla_kv_norm[l].reshape(1, -1), wk, wv, place, tm=512)
        y_b = _flash(dq, dk, proj, OFF_DV, batch, seq, DIFF_HEADS, n_maps=2, tq=ATTN_TQ, tk=ATTN_TK,
                     lam_init=lam_init, lamv=diff_lambda[l], subln=diff_subln[l].reshape(1, -1),
                     name="diff_attn")
        y_c = _flash(mq, mk, mv, 0, batch, seq, MLA_HEADS, n_maps=1, tq=ATTN_TQ, tk=ATTN_TK, name="mla_attn")
        h = _merge(proj, y_b, y_c, h, conv_w[l], w_branch[l].astype(bf), w_out[l].astype(bf), seq, tm=256)

        memkv = _norm_matmul(mem2, norm_mem[l], w_ckv[l].astype(bf), tm=256, tn=512, name="mem_kv")
        h = _cross(h, norm_cross[l].reshape(1, d), w_cq[l].astype(bf), memkv, w_co[l].astype(bf),
                   seq, mem_tokens, tm=256)

        xn, idx, gate = _route(h, norm_ffn[l].reshape(1, d), peer_w_q[l].astype(bf),
                               peer_keys[l].astype(bf), tm=256)
        uv = _pack_tables(peer_u[l], peer_v[l])
        h = _experts(idx.T, gate.T, xn, h, uv, tt=tt)

    out = _final_norm(h, final_norm, tm=512)
    return out.reshape(batch, seq, d)
```

```python
import functools
import math

import numpy as np
import jax
import jax.numpy as jnp
from jax import lax
from jax.experimental import pallas as pl
from jax.experimental.pallas import tpu as pltpu

EPS = 1e-6
ROPE_THETA = 500000.0
CONV_CH = 512
DIFF_HEADS = 4
DIFF_HEAD_DIM = 64
DIFF_ROT = DIFF_HEAD_DIM // 4
MLA_HEADS = 4
MLA_Q_RANK = 256
MLA_KV_RANK = 128
MLA_NOPE = 64
MLA_ROPE = 32
MLA_V = 128
N_BRANCH = 3
CROSS_HEADS = 4
PEER_HEADS = 8
PEER_N_KEYS = 128
PEER_KEY_DIM = 128
PEER_TOPK = 16

LANES = 128
SUBLANES = 8
VMEM_LIMIT_BYTES = 56 * 1024 * 1024

ATTN_TQ = 512

HEAD_W = 128
OFF_GATES = 0
OFF_AB = 3072
OFF_AC = OFF_AB + CONV_CH
OFF_AH = OFF_AC + CONV_CH
OFF_DQ = OFF_AH + CONV_CH
OFF_DK = OFF_DQ + 512
OFF_DV = OFF_DK + 512
OFF_CQ = OFF_DV + 512
OFF_CKV = OFF_CQ + MLA_Q_RANK
OFF_KR = OFF_CKV + MLA_KV_RANK
PROJ_W = OFF_KR + LANES


def _cparams(*sem):
    return pltpu.CompilerParams(dimension_semantics=sem, vmem_limit_bytes=VMEM_LIMIT_BYTES)


def _norm_matmul_kernel(x_ref, g_ref, w_ref, o_ref, xn_ref):
    @pl.when(pl.program_id(1) == 0)
    def _():
        x = x_ref[...]
        ms = jnp.mean(x * x, axis=-1, keepdims=True)
        xn_ref[...] = (x * lax.rsqrt(ms + EPS) * g_ref[...]).astype(xn_ref.dtype)

    o_ref[...] = jnp.dot(xn_ref[...], w_ref[...],
                         preferred_element_type=jnp.float32).astype(o_ref.dtype)


def _norm_matmul(x, g, w, *, tm, tn, name):
    t, k = x.shape
    n = w.shape[1]
    tm = min(tm, t)
    return pl.pallas_call(
        _norm_matmul_kernel,
        out_shape=jax.ShapeDtypeStruct((t, n), jnp.bfloat16),
        grid=(t // tm, n // tn),
        in_specs=[pl.BlockSpec((tm, k), lambda i, j: (i, 0)),
                  pl.BlockSpec((1, k), lambda i, j: (0, 0)),
                  pl.BlockSpec((k, tn), lambda i, j: (0, j))],
        out_specs=pl.BlockSpec((tm, tn), lambda i, j: (i, j)),
        scratch_shapes=[pltpu.VMEM((tm, k), jnp.bfloat16)],
        compiler_params=_cparams("parallel", "arbitrary"),
        name=name,
    )(x, g.reshape(1, k), w)


def _rope_block(xb, c, s, lo_mask, shift):
    partner = jnp.where(lo_mask, pltpu.roll(xb, LANES - shift, axis=1), pltpu.roll(xb, shift, axis=1))
    return xb * c + partner * s


def _rms(x, g):
    ms = jnp.mean(x * x, axis=-1, keepdims=True)
    return x * lax.rsqrt(ms + EPS) * g


def _prep_kernel(dq_ref, dk_ref, cq_ref, ckv_ref, kr_ref, cd_ref, sd_ref, cm_ref, sm_ref,
                 qn_ref, wuq_ref, kvn_ref, wk_ref, wv_ref, place_ref,
                 dq_o, dk_o, mq_o, mk_o, mv_o):
    f32 = jnp.float32
    tm = dq_ref.shape[0]
    lane = lax.broadcasted_iota(jnp.int32, (tm, LANES), 1)
    cd, sd, cm, sm = cd_ref[...], sd_ref[...], cm_ref[...], sm_ref[...]
    d_lo = (lane % DIFF_HEAD_DIM) < (DIFF_ROT // 2)
    m_lo = lane < (MLA_NOPE + MLA_ROPE // 2)
    d_scale = DIFF_HEAD_DIM ** -0.5
    m_scale = (MLA_NOPE + MLA_ROPE) ** -0.5
    first_half = lane < DIFF_HEAD_DIM

    for h in range(DIFF_HEADS):
        sl = slice(h * HEAD_W, (h + 1) * HEAD_W)
        qb = _rope_block(dq_ref[:, sl].astype(f32), cd, sd, d_lo, DIFF_ROT // 2) * d_scale
        kb = _rope_block(dk_ref[:, sl].astype(f32), cd, sd, d_lo, DIFF_ROT // 2)
        dq_o[:, 2 * h * HEAD_W:(2 * h + 1) * HEAD_W] = jnp.where(first_half, qb, 0.0).astype(dq_o.dtype)
        dq_o[:, (2 * h + 1) * HEAD_W:(2 * h + 2) * HEAD_W] = jnp.where(first_half, 0.0, qb).astype(dq_o.dtype)
        dk_o[:, sl] = kb.astype(dk_o.dtype)

    cqn = _rms(cq_ref[...].astype(f32), qn_ref[...]).astype(jnp.bfloat16)
    q = jnp.dot(cqn, wuq_ref[...], preferred_element_type=f32)
    ckvn = _rms(ckv_ref[...].astype(f32), kvn_ref[...]).astype(jnp.bfloat16)
    k = jnp.dot(ckvn, wk_ref[...], preferred_element_type=f32)
    k = k + jnp.dot(kr_ref[...], place_ref[...], preferred_element_type=f32)
    v = jnp.dot(ckvn, wv_ref[...], preferred_element_type=f32)
    for h in range(MLA_HEADS):
        sl = slice(h * HEAD_W, (h + 1) * HEAD_W)
        mq_o[:, sl] = (_rope_block(q[:, sl], cm, sm, m_lo, MLA_ROPE // 2) * m_scale).astype(mq_o.dtype)
        mk_o[:, sl] = _rope_block(k[:, sl], cm, sm, m_lo, MLA_ROPE // 2).astype(mk_o.dtype)
    mv_o[...] = v.astype(mv_o.dtype)


def _prep(proj, cd, sd, cm, sm, qn, wuq, kvn, wk, wv, place, *, tm):
    t = proj.shape[0]
    tm = min(tm, t)
    row = lambda w, off: pl.BlockSpec((tm, w), lambda i, _o=off // w: (i, _o))
    full = lambda a: pl.BlockSpec(a.shape, lambda i: (0,) * a.ndim)
    tab = pl.BlockSpec((tm, LANES), lambda i: (i, 0))
    bf = jnp.bfloat16
    return pl.pallas_call(
        _prep_kernel,
        out_shape=[jax.ShapeDtypeStruct((t, 2 * DIFF_HEADS * HEAD_W), bf),
                   jax.ShapeDtypeStruct((t, DIFF_HEADS * HEAD_W), bf),
                   jax.ShapeDtypeStruct((t, MLA_HEADS * HEAD_W), bf),
                   jax.ShapeDtypeStruct((t, MLA_HEADS * HEAD_W), bf),
                   jax.ShapeDtypeStruct((t, MLA_HEADS * HEAD_W), bf)],
        grid=(t // tm,),
        in_specs=[row(512, OFF_DQ), row(512, OFF_DK), row(MLA_Q_RANK, OFF_CQ),
                  row(MLA_KV_RANK, OFF_CKV), row(LANES, OFF_KR), tab, tab, tab, tab,
                  full(qn), full(wuq), full(kvn), full(wk), full(wv), full(place)],
        out_specs=[pl.BlockSpec((tm, 2 * DIFF_HEADS * HEAD_W), lambda i: (i, 0)),
                   pl.BlockSpec((tm, DIFF_HEADS * HEAD_W), lambda i: (i, 0)),
                   pl.BlockSpec((tm, MLA_HEADS * HEAD_W), lambda i: (i, 0)),
                   pl.BlockSpec((tm, MLA_HEADS * HEAD_W), lambda i: (i, 0)),
                   pl.BlockSpec((tm, MLA_HEADS * HEAD_W), lambda i: (i, 0))],
        compiler_params=_cparams("parallel"),
        name="attn_prep",
    )(proj, proj, proj, proj, proj, cd, sd, cm, sm, qn, wuq, kvn, wk, wv, place)


def _attn_kernel(*refs, n_maps, lam_init):
    if n_maps == 2:
        qt_ref, k_ref, vt_ref, lamv_ref, subln_ref, o_ref = refs
    else:
        qt_ref, k_ref, vt_ref, o_ref = refs
    f32 = jnp.float32
    k = k_ref[...]
    vt = vt_ref[...]
    outs = []
    for mi in range(n_maps):
        qt = qt_ref[mi * HEAD_W:(mi + 1) * HEAD_W, :]
        st = jnp.dot(k, qt, preferred_element_type=f32)
        pt = jnp.exp(st - jnp.max(st, axis=0, keepdims=True))
        l = jnp.sum(pt, axis=0, keepdims=True)
        outs.append(jnp.dot(vt, pt.astype(vt.dtype), preferred_element_type=f32) / l)
    if n_maps == 2:
        lv = lamv_ref[...]
        lam = (jnp.exp(jnp.sum(lv[0:1] * lv[1:2], axis=-1, keepdims=True))
               - jnp.exp(jnp.sum(lv[2:3] * lv[3:4], axis=-1, keepdims=True)) + lam_init)
        o = outs[0] - lam * outs[1]
        ms = jnp.mean(o * o, axis=0, keepdims=True)
        o = o * lax.rsqrt(ms + EPS) * subln_ref[...] * (1.0 - lam_init)
    else:
        o = outs[0]
    o_ref[...] = o.astype(o_ref.dtype)


def _attn(qt, k, vt, batch, seq, heads, *, n_maps, tq, lam_init=0.0, lamv=None, subln=None, name):
    tq = min(tq, seq)
    nq = seq // tq
    in_specs = [pl.BlockSpec((n_maps * HEAD_W, tq), lambda b, h, i: (h, b * nq + i)),
                pl.BlockSpec((seq, HEAD_W), lambda b, h, i: (b, h)),
                pl.BlockSpec((HEAD_W, seq), lambda b, h, i: (h, b))]
    args = [qt, k, vt]
    if n_maps == 2:
        in_specs += [pl.BlockSpec(lamv.shape, lambda b, h, i: (0, 0)),
                     pl.BlockSpec(subln.shape, lambda b, h, i: (0, 0))]
        args += [lamv, subln]
    return pl.pallas_call(
        functools.partial(_attn_kernel, n_maps=n_maps, lam_init=lam_init),
        out_shape=jax.ShapeDtypeStruct((heads * HEAD_W, batch * seq), jnp.bfloat16),
        grid=(batch, heads, nq),
        in_specs=in_specs,
        out_specs=pl.BlockSpec((HEAD_W, tq), lambda b, h, i: (h, b * nq + i)),
        compiler_params=_cparams("parallel", "parallel", "arbitrary"),
        name=name,
    )(*args)


def _sigmoid(x):
    return 1.0 / (1.0 + jnp.exp(-x))


def _merge_kernel(g0_ref, g1_ref, g2_ref, ab_ref, ac_ref, ah_ref, pc_ref, ph_ref, nc_ref, nh_ref,
                  yb_ref, yc_ref, h_ref, cw_ref, wb_ref, wo_ref, o_ref, *, tiles_per_seq):
    f32 = jnp.float32
    tm = ab_ref.shape[0]
    i = pl.program_id(0)
    pos = i % tiles_per_seq
    z = ac_ref[...].astype(f32) * ah_ref[...].astype(f32)
    zp_edge = pc_ref[SUBLANES - 1:SUBLANES, :].astype(f32) * ph_ref[SUBLANES - 1:SUBLANES, :].astype(f32)
    zn_edge = nc_ref[0:1, :].astype(f32) * nh_ref[0:1, :].astype(f32)
    zp_edge = jnp.where(pos > 0, zp_edge, 0.0)
    zn_edge = jnp.where(pos < tiles_per_seq - 1, zn_edge, 0.0)
    row = lax.broadcasted_iota(jnp.int32, z.shape, 0)
    zp = jnp.where(row == 0, zp_edge, pltpu.roll(z, 1, axis=0))
    zn = jnp.where(row == tm - 1, zn_edge, pltpu.roll(z, tm - 1, axis=0))
    cw = cw_ref[...]
    y_a = ab_ref[...].astype(f32) * (cw[0:1] * zp + cw[1:2] * z + cw[2:3] * zn)

    bf = jnp.bfloat16
    merged = _sigmoid(g0_ref[...].astype(f32)) * jnp.dot(y_a.astype(bf), wb_ref[0], preferred_element_type=f32)
    merged += _sigmoid(g1_ref[...].astype(f32)) * jnp.dot(yb_ref[...], wb_ref[1], preferred_element_type=f32)
    merged += _sigmoid(g2_ref[...].astype(f32)) * jnp.dot(yc_ref[...], wb_ref[2], preferred_element_type=f32)
    o_ref[...] = h_ref[...] + jnp.dot(merged.astype(bf), wo_ref[...], preferred_element_type=f32)


def _merge(proj, y_b, y_c, h, conv_w, w_branch, w_out, seq, *, tm):
    t, d = h.shape
    tm = min(tm, seq)
    tiles_per_seq = seq // tm
    r8 = tm // SUBLANES
    nrow8 = t // SUBLANES
    row = lambda w, off: pl.BlockSpec((tm, w), lambda i, _o=off // w: (i, _o))
    prev = lambda off: pl.BlockSpec((SUBLANES, CONV_CH),
                                    lambda i, _o=off // CONV_CH: (jnp.maximum(i * r8 - 1, 0), _o))
    nxt = lambda off: pl.BlockSpec((SUBLANES, CONV_CH),
                                   lambda i, _o=off // CONV_CH: (jnp.minimum((i + 1) * r8, nrow8 - 1), _o))
    full = lambda a: pl.BlockSpec(a.shape, lambda i: (0,) * a.ndim)
    return pl.pallas_call(
        functools.partial(_merge_kernel, tiles_per_seq=tiles_per_seq),
        out_shape=jax.ShapeDtypeStruct((t, d), jnp.float32),
        grid=(t // tm,),
        in_specs=[row(d, OFF_GATES), row(d, OFF_GATES + d), row(d, OFF_GATES + 2 * d),
                  row(CONV_CH, OFF_AB), row(CONV_CH, OFF_AC), row(CONV_CH, OFF_AH),
                  prev(OFF_AC), prev(OFF_AH), nxt(OFF_AC), nxt(OFF_AH),
                  pl.BlockSpec((tm, y_b.shape[1]), lambda i: (i, 0)),
                  pl.BlockSpec((tm, y_c.shape[1]), lambda i: (i, 0)),
                  pl.BlockSpec((tm, d), lambda i: (i, 0)),
                  full(conv_w), full(w_branch), full(w_out)],
        out_specs=pl.BlockSpec((tm, d), lambda i: (i, 0)),
        compiler_params=_cparams("parallel"),
        name="gated_merge",
    )(proj, proj, proj, proj, proj, proj, proj, proj, proj, proj, y_b, y_c, h, conv_w, w_branch, w_out)


def _cross_kernel(h_ref, g_ref, wq_ref, kv_ref, wo_ref, o_ref):
    f32, bf = jnp.float32, jnp.bfloat16
    h = h_ref[...]
    d = h.shape[1]
    hd = d // CROSS_HEADS
    hn = _rms(h, g_ref[...]).astype(bf)
    q = (jnp.dot(hn, wq_ref[...], preferred_element_type=f32) * (hd ** -0.5)).astype(bf)
    outs = []
    for hh in range(CROSS_HEADS):
        kh = kv_ref[:, hh * hd:(hh + 1) * hd]
        vh = kv_ref[:, d + hh * hd:d + (hh + 1) * hd]
        s = lax.dot_general(q[:, hh * hd:(hh + 1) * hd], kh, (((1,), (1,)), ((), ())),
                            preferred_element_type=f32)
        p = jnp.exp(s - jnp.max(s, axis=-1, keepdims=True))
        p = p / jnp.sum(p, axis=-1, keepdims=True)
        outs.append(jnp.dot(p.astype(bf), vh, preferred_element_type=f32).astype(bf))
    o = jnp.concatenate(outs, axis=-1)
    o_ref[...] = h + jnp.dot(o, wo_ref[...], preferred_element_type=f32)


def _cross(h, g, w_cq, memkv, w_co, seq, mem_tokens, *, tm):
    t, d = h.shape
    tm = min(tm, seq)
    tiles_per_seq = seq // tm
    full = lambda a: pl.BlockSpec(a.shape, lambda i: (0,) * a.ndim)
    return pl.pallas_call(
        _cross_kernel,
        out_shape=jax.ShapeDtypeStruct((t, d), jnp.float32),
        grid=(t // tm,),
        in_specs=[pl.BlockSpec((tm, d), lambda i: (i, 0)), full(g), full(w_cq),
                  pl.BlockSpec((mem_tokens, 2 * d), lambda i: (i // tiles_per_seq, 0)), full(w_co)],
        out_specs=pl.BlockSpec((tm, d), lambda i: (i, 0)),
        compiler_params=_cparams("parallel"),
        name="cross_attn",
    )(h, g, w_cq, memkv, w_co)


def _topk_rows(sc, k, vals_ref, idx_ref, payload=None):
    n = sc.shape[0]
    row = lax.broadcasted_iota(jnp.int32, sc.shape, 0).astype(jnp.float32)
    for r in range(k):
        m = jnp.max(sc, axis=0, keepdims=True)
        first = jnp.min(jnp.where(sc == m, row, float(n)), axis=0, keepdims=True)
        hit = row == first
        vals_ref[r:r + 1, :] = m
        if payload is None:
            idx_ref[r:r + 1, :] = first
        else:
            idx_ref[r:r + 1, :] = jnp.sum(jnp.where(hit, payload, 0.0), axis=0, keepdims=True)
        sc = jnp.where(hit, -jnp.inf, sc)


def _route_kernel(h_ref, g_ref, wq_ref, keys_ref, xn_o, idx_o, gate_o,
                  s1_ref, i1_ref, s2_ref, i2_ref, cs_ref, ci_ref, ts_ref, ti_ref):
    f32, bf = jnp.float32, jnp.bfloat16
    xn = _rms(h_ref[...], g_ref[...])
    xn_o[...] = xn.astype(xn_o.dtype)
    q = jnp.dot(xn.astype(bf), wq_ref[...], preferred_element_type=f32).astype(bf)
    kd = PEER_KEY_DIM
    for hh in range(PEER_HEADS):
        for p, (s_ref, i_ref) in enumerate(((s1_ref, i1_ref), (s2_ref, i2_ref))):
            c0 = (hh * 2 + p) * kd
            sc = lax.dot_general(keys_ref[hh, p], q[:, c0:c0 + kd], (((1,), (1,)), ((), ())),
                                 preferred_element_type=f32)
            _topk_rows(sc, PEER_TOPK, s_ref, i_ref)
        nk = float(PEER_N_KEYS)
        cs_ref[0:PEER_TOPK, :] = s1_ref[0:1, :] + s2_ref[...]
        ci_ref[0:PEER_TOPK, :] = i1_ref[0:1, :] * nk + i2_ref[...]
        off = PEER_TOPK
        for a in range(1, PEER_TOPK // 2):
            cs_ref[off:off + SUBLANES, :] = s1_ref[a:a + 1, :] + s2_ref[0:SUBLANES, :]
            ci_ref[off:off + SUBLANES, :] = i1_ref[a:a + 1, :] * nk + i2_ref[0:SUBLANES, :]
            off += SUBLANES
        cs_ref[off:off + SUBLANES, :] = s1_ref[PEER_TOPK // 2:PEER_TOPK, :] + s2_ref[0:1, :]
        ci_ref[off:off + SUBLANES, :] = i1_ref[PEER_TOPK // 2:PEER_TOPK, :] * nk + i2_ref[0:1, :]
        _topk_rows(cs_ref[...], PEER_TOPK, ts_ref, ti_ref, payload=ci_ref[...])
        ts = ts_ref[...]
        e = jnp.exp(ts - ts[0:1, :])
        gate_o[hh * PEER_TOPK:(hh + 1) * PEER_TOPK, :] = e / jnp.sum(e, axis=0, keepdims=True)
        idx_o[hh * PEER_TOPK:(hh + 1) * PEER_TOPK, :] = ti_ref[...].astype(jnp.int32)


def _route(h, g, w_q, keys, *, tm):
    t, d = h.shape
    tm = min(tm, t)
    nj = PEER_HEADS * PEER_TOPK
    full = lambda a: pl.BlockSpec(a.shape, lambda i: (0,) * a.ndim)
    assert PEER_TOPK == 2 * SUBLANES
    n_cand = PEER_TOPK + (PEER_TOPK // 2) * SUBLANES
    small = lambda: pltpu.VMEM((PEER_TOPK, tm), jnp.float32)
    big = lambda: pltpu.VMEM((n_cand, tm), jnp.float32)
    return pl.pallas_call(
        _route_kernel,
        out_shape=[jax.ShapeDtypeStruct((t, d), jnp.bfloat16),
                   jax.ShapeDtypeStruct((nj, t), jnp.int32),
                   jax.ShapeDtypeStruct((nj, t), jnp.float32)],
        grid=(t // tm,),
        in_specs=[pl.BlockSpec((tm, d), lambda i: (i, 0)), full(g), full(w_q), full(keys)],
        out_specs=[pl.BlockSpec((tm, d), lambda i: (i, 0)),
                   pl.BlockSpec((nj, tm), lambda i: (0, i)),
                   pl.BlockSpec((nj, tm), lambda i: (0, i))],
        scratch_shapes=[small(), small(), small(), small(), big(), big(), small(), small()],
        compiler_params=_cparams("parallel"),
        name="peer_route",
    )(h, g, w_q, keys)


def _gelu(x):
    return 0.5 * x * (1.0 + lax.erf(x * (2.0 ** -0.5)))


def _transpose_tiles(x):
    sub = lax.broadcasted_iota(jnp.int32, (1, SUBLANES, LANES), 1)
    parts = [x[:, r] for r in range(SUBLANES)]
    for s in (4, 2, 1):
        low = (sub & s) == 0
        new = []
        for r in range(SUBLANES):
            other = parts[r ^ s]
            if r & s == 0:
                new.append(jnp.where(low, parts[r], pltpu.roll(other, s, axis=1)))
            else:
                new.append(jnp.where(low, pltpu.roll(other, SUBLANES - s, axis=1), parts[r]))
        parts = new
    return jnp.stack(parts, axis=1)


def _experts_kernel(idx_ref, idxn_ref, gate_ref, xn_ref, h_ref, uv_hbm, o_ref, buf, tbuf, sem):
    f32, bf = jnp.float32, jnp.bfloat16
    tt = xn_ref.shape[0] // 2
    d = xn_ref.shape[1]
    nj = gate_ref.shape[1]
    nrows = tt * nj
    nchunk = d // LANES
    ngroups = nrows // SUBLANES
    i = pl.program_id(0)
    n = pl.num_programs(0)

    def start_group(iref, base, sl, g):
        for k in range(SUBLANES):
            r = g * SUBLANES + k
            e = iref[0, 0, base + r]
            pltpu.make_async_copy(uv_hbm.at[e], buf.at[sl, r], sem.at[sl]).start(priority=k % 2)

    def wait_slot(sl):
        pltpu.make_async_copy(buf.at[sl], buf.at[sl], sem.at[sl]).wait()

    @pl.when(i == 0)
    def _():
        def body(g, carry):
            start_group(idx_ref, 0, 0, g)
            return carry
        lax.fori_loop(0, ngroups, body, 0)

    slab_groups = 4
    n_slabs = ngroups // slab_groups
    row_blocks = 4
    rb_groups = ngroups // row_blocks
    rb_rows = rb_groups * SUBLANES
    starts_per_slab = 3
    starts_per_piece = -(-(ngroups - starts_per_slab * n_slabs) // (row_blocks * nchunk))
    col = lax.broadcasted_iota(jnp.int32, (tt, nrows), 1)
    row = lax.broadcasted_iota(jnp.int32, (tt, nrows), 0)
    own = (col >= row * nj) & (col < (row + 1) * nj)

    def block(sl, next_iref, next_base):
        tok = slice(sl * tt, (sl + 1) * tt)
        pending = iter(range(ngroups))

        def start_groups(count):
            for _ in range(count):
                g = next(pending, None)
                if g is not None:
                    start_group(next_iref, next_base, 1 - sl, g)

        def words(rb, c):
            return tbuf[rb * rb_groups:(rb + 1) * rb_groups, c, :, :].reshape(rb_rows, LANES)

        wait_slot(sl)
        for j in range(n_slabs):
            rows = slice(j * slab_groups * SUBLANES, (j + 1) * slab_groups * SUBLANES)
            tiles = buf[sl, rows].reshape(slab_groups, SUBLANES, nchunk, LANES)
            tbuf[j * slab_groups:(j + 1) * slab_groups] = _transpose_tiles(tiles)
            start_groups(starts_per_slab)
        xn = xn_ref[tok, :].astype(bf)
        acts = []
        for rb in range(row_blocks):
            a = jnp.zeros((tt, rb_rows), f32)
            for c in range(nchunk):
                uc = lax.bitcast_convert_type(words(rb, c) & jnp.uint32(0xFFFF0000), f32).astype(bf)
                a += lax.dot_general(xn[:, c * LANES:(c + 1) * LANES], uc, (((1,), (1,)), ((), ())),
                                     preferred_element_type=f32)
                start_groups(starts_per_piece)
            acts.append(a)
        start_groups(ngroups)
        act = jnp.concatenate(acts, axis=1)
        gate = jnp.concatenate([gate_ref[tok, :]] * tt, axis=1)
        w = jnp.where(own, gate * _gelu(act), 0.0).astype(bf)
        for c in range(nchunk):
            acc = None
            for rb in range(row_blocks):
                vc = lax.bitcast_convert_type(words(rb, c) << 16, f32).astype(bf)
                part = jnp.dot(w[:, rb * rb_rows:(rb + 1) * rb_rows], vc, preferred_element_type=f32)
                acc = part if acc is None else acc + part
            cs = slice(c * LANES, (c + 1) * LANES)
            o_ref[tok, cs] = h_ref[tok, cs] + acc

    block(0, idx_ref, nrows)
    block(1, idxn_ref, 0)

    @pl.when(i == n - 1)
    def _():
        wait_slot(0)


def _experts(idx, gate, xn, h, uv, *, tt):
    t, d = h.shape
    nj = gate.shape[1]
    n = t // (2 * tt)
    nrows = tt * nj
    nchunk = d // LANES
    assert nchunk == SUBLANES
    idx_step = idx.reshape(n, 1, 2 * nrows)
    idx_block = idx.reshape(2 * n, 1, nrows)
    return pl.pallas_call(
        _experts_kernel,
        out_shape=jax.ShapeDtypeStruct((t, d), jnp.float32),
        grid=(n,),
        in_specs=[pl.BlockSpec((1, 1, 2 * nrows), lambda i: (i, 0, 0), memory_space=pltpu.SMEM),
                  pl.BlockSpec((1, 1, nrows), lambda i: (jnp.minimum(2 * i + 2, 2 * n - 2), 0, 0),
                               memory_space=pltpu.SMEM),
                  pl.BlockSpec((2 * tt, nj), lambda i: (i, 0)),
                  pl.BlockSpec((2 * tt, d), lambda i: (i, 0)),
                  pl.BlockSpec((2 * tt, d), lambda i: (i, 0)),
                  pl.BlockSpec(memory_space=pl.ANY)],
        out_specs=pl.BlockSpec((2 * tt, d), lambda i: (i, 0)),
        scratch_shapes=[pltpu.VMEM((2, nrows, nchunk, LANES), jnp.uint32),
                        pltpu.VMEM((nrows // SUBLANES, nchunk, SUBLANES, LANES), jnp.uint32),
                        pltpu.SemaphoreType.DMA((2,))],
        compiler_params=_cparams("arbitrary"),
        name="peer_experts",
    )(idx_step, idx_block, gate, xn, h, uv)


def _final_norm_kernel(x_ref, g_ref, o_ref):
    o_ref[...] = _rms(x_ref[...], g_ref[...])


def _final_norm(h, g, *, tm):
    t, d = h.shape
    tm = min(tm, t)
    return pl.pallas_call(
        _final_norm_kernel,
        out_shape=jax.ShapeDtypeStruct((t, d), jnp.float32),
        grid=(t // tm,),
        in_specs=[pl.BlockSpec((tm, d), lambda i: (i, 0)), pl.BlockSpec((1, d), lambda i: (0, 0))],
        out_specs=pl.BlockSpec((tm, d), lambda i: (i, 0)),
        compiler_params=_cparams("parallel"),
        name="final_norm",
    )(h, g.reshape(1, d))


def _rope_lane_tables(positions, rot_dim, period, lane0):
    half = rot_dim // 2
    inv = 1.0 / (ROPE_THETA ** (jnp.arange(0, rot_dim, 2, dtype=jnp.float32) / rot_dim))
    ang = positions.astype(jnp.float32).reshape(-1, 1) * inv[None, :]
    cos, sin = jnp.cos(ang), jnp.sin(ang)
    lane = np.arange(LANES) % period - lane0
    in_rot = (lane >= 0) & (lane < rot_dim)
    sel = np.where(in_rot, lane % half, 0)
    sign = np.where(lane < half, -1.0, 1.0).astype(np.float32)
    c = jnp.where(in_rot[None, :], cos[:, sel], 1.0)
    s = jnp.where(in_rot[None, :], sin[:, sel] * sign[None, :], 0.0)
    return c, s


def _layer_weights(l, w_in, mla_w_uq, mla_w_ukv):
    bf = jnp.bfloat16
    d = w_in.shape[1]
    offs = np.cumsum([0, CONV_CH, CONV_CH, CONV_CH, 512, 512, 512, MLA_Q_RANK, MLA_KV_RANK, MLA_ROPE])
    w = w_in[l]
    seg = lambda i: w[:, int(offs[i]):int(offs[i + 1])]
    gates = w[:, int(offs[9]):]
    kr = jnp.pad(seg(8), ((0, 0), (0, LANES - MLA_ROPE)))
    w_in_p = jnp.concatenate([gates, seg(0), seg(1), seg(2), seg(3), seg(4), seg(5), seg(6), seg(7), kr],
                             axis=1).astype(bf)
    qd = MLA_NOPE + MLA_ROPE
    wuq = mla_w_uq[l].reshape(MLA_Q_RANK, MLA_HEADS, qd)
    wuq = jnp.pad(wuq, ((0, 0), (0, 0), (0, HEAD_W - qd))).reshape(MLA_Q_RANK, MLA_HEADS * HEAD_W).astype(bf)
    wukv = mla_w_ukv[l].reshape(MLA_KV_RANK, MLA_HEADS, MLA_NOPE + MLA_V)
    wk = jnp.pad(wukv[:, :, :MLA_NOPE], ((0, 0), (0, 0), (0, HEAD_W - MLA_NOPE)))
    wk = wk.reshape(MLA_KV_RANK, MLA_HEADS * HEAD_W).astype(bf)
    wv = wukv[:, :, MLA_NOPE:].reshape(MLA_KV_RANK, MLA_HEADS * MLA_V).astype(bf)
    return w_in_p, wuq, wk, wv


def _pack_tables(u, v):
    hi = lax.bitcast_convert_type(u.astype(jnp.bfloat16), jnp.uint16).astype(jnp.uint32)
    lo = lax.bitcast_convert_type(v.astype(jnp.bfloat16), jnp.uint16).astype(jnp.uint32)
    return ((hi << 16) | lo).reshape(u.shape[0], u.shape[1] // LANES, LANES)


def _placement():
    e = np.zeros((LANES, MLA_HEADS * HEAD_W), np.float32)
    for h in range(MLA_HEADS):
        for c in range(MLA_ROPE):
            e[c, h * HEAD_W + MLA_NOPE + c] = 1.0
    return jnp.asarray(e, jnp.bfloat16)


def kernel(x, mem, positions, norm_mix, w_in, conv_w, diff_lambda, diff_subln, mla_q_norm, mla_w_uq,
           mla_kv_norm, mla_w_ukv, w_branch, w_out, norm_cross, norm_mem, w_cq, w_ckv, w_co, norm_ffn,
           peer_w_q, peer_keys, peer_u, peer_v, final_norm):
    bf = jnp.bfloat16
    batch, seq, d = x.shape
    mem_tokens = mem.shape[1]
    depth = w_in.shape[0]
    t = batch * seq
    h = x.reshape(t, d)
    mem2 = mem.reshape(batch * mem_tokens, d)

    cd, sd = _rope_lane_tables(positions, DIFF_ROT, DIFF_HEAD_DIM, 0)
    cm, sm = _rope_lane_tables(positions, MLA_ROPE, LANES, MLA_NOPE)
    place = _placement()
    tt = 16

    for l in range(depth):
        lam_init = 0.8 - 0.6 * math.exp(-0.3 * l)
        w_in_p, wuq, wk, wv = _layer_weights(l, w_in, mla_w_uq, mla_w_ukv)

        proj = _norm_matmul(h, norm_mix[l], w_in_p, tm=512, tn=PROJ_W // 2, name="in_proj")
        dq, dk, mq, mk, mv = _prep(proj, cd, sd, cm, sm, mla_q_norm[l].reshape(1, -1), wuq,
                                   mla_kv_norm[l].reshape(1, -1), wk, wv, place, tm=512)
        y_b = _attn(dq.T, dk, proj[:, OFF_DV:OFF_DV + DIFF_HEADS * HEAD_W].T, batch, seq, DIFF_HEADS, n_maps=2,
                    tq=ATTN_TQ, lam_init=lam_init, lamv=diff_lambda[l], subln=diff_subln[l].reshape(-1, 1),
                    name="diff_attn").T
        y_c = _attn(mq.T, mk, mv.T, batch, seq, MLA_HEADS, n_maps=1, tq=ATTN_TQ, name="mla_attn").T
        h = _merge(proj, y_b, y_c, h, conv_w[l], w_branch[l].astype(bf), w_out[l].astype(bf), seq, tm=512)

        memkv = _norm_matmul(mem2, norm_mem[l], w_ckv[l].astype(bf), tm=256, tn=512, name="mem_kv")
        h = _cross(h, norm_cross[l].reshape(1, d), w_cq[l].astype(bf), memkv, w_co[l].astype(bf),
                   seq, mem_tokens, tm=512)

        xn, idx, gate = _route(h, norm_ffn[l].reshape(1, d), peer_w_q[l].astype(bf),
                               peer_keys[l].astype(bf), tm=256)
        uv = _pack_tables(peer_u[l], peer_v[l])
        h = _experts(idx.T, gate.T, xn, h, uv, tt=tt)

    out = _final_norm(h, final_norm, tm=512)
    return out.reshape(batch, seq, d)
```
